```python
import math
import jax, jax.numpy as jnp
from jax import lax
import numpy as np


D_MODEL = 1024
BATCH = 8
SEQ = 2048
DEPTH = 4
DEC_BATCH = 128
DEC_SEQ = 8
PAST_LEN = 2048
PAGE_SIZE = 128

N_MIXERS = 2
N_A_LAYERS = (DEPTH + 1) // 2
N_B_LAYERS = DEPTH // 2
EXPAND = 2
E_A = EXPAND * D_MODEL
A_GROUPS = 16
A_GROUP_DIM = E_A // A_GROUPS
CHUNK = 128
B_HEADS = 16
HEAD_DIM = D_MODEL // B_HEADS
E_B = B_HEADS * HEAD_DIM
MOBA_BLOCK = 256
MOBA_TOPK = 3
N_BUCKETS = 32
MAX_DISTANCE = 128
QUERY_GATHER_BUDGET = 128
EPS = 1e-6

kernel_name = 'hybrid_gmlp_moba_decode_step'


def rms_norm(x, g):
    xf = x.astype(jnp.float32)
    y = xf * lax.rsqrt(jnp.mean(xf * xf, axis=-1, keepdims=True) + EPS)
    return (y * g.astype(jnp.float32)).astype(x.dtype)


def t5_bucket(dist):
    n = jnp.maximum(dist, 0)
    max_exact = N_BUCKETS // 2
    nf = jnp.maximum(n, 1).astype(jnp.float32)
    large = max_exact + (jnp.log(nf / max_exact) / math.log(MAX_DISTANCE / max_exact)
                         * (N_BUCKETS - max_exact)).astype(jnp.int32)
    return jnp.where(n < max_exact, n, jnp.minimum(large, N_BUCKETS - 1))


def chunk_mix(v, w_s, b_s):
    B, T, G, c = v.shape
    L = min(T, CHUNK)
    n = -(-T // L)
    vp = jnp.pad(v, ((0, 0), (0, n * L - T), (0, 0), (0, 0))).reshape(B, n, L, G, c)
    w = jnp.tril(w_s[:, :L, :L]).astype(v.dtype)
    s = jnp.einsum('gts,bnsgc->bntgc', w, vp) + b_s[:, :L].T[None, None, :, :, None]
    return s.reshape(B, n * L, G, c)[:, :T]


def mixer_a(h, w_in, g_v, w_s, b_s, w_out):
    B, T, _ = h.shape
    u, v, z = jnp.split(h @ w_in, 3, axis=-1)
    u = jax.nn.gelu(u, approximate=False)
    v = rms_norm(jax.nn.gelu(v, approximate=False), g_v)
    s = chunk_mix(v.reshape(B, T, A_GROUPS, A_GROUP_DIM), w_s, b_s).reshape(B, T, E_A)
    y = (u * s * jax.nn.silu(z)) @ w_out
    return y, v


def _query_block(batch, t):
    limit = max(1, min(CHUNK, QUERY_GATHER_BUDGET // batch))
    qb = 1
    while qb * 2 <= limit and t % (qb * 2) == 0:
        qb *= 2
    return qb


def moba_attention(q, k_full, v_full, q_pos0, rel_bias):
    B, T, H, d = q.shape
    S = k_full.shape[1]
    nb = -(-S // MOBA_BLOCK)
    pad = nb * MOBA_BLOCK - S
    k_pad = jnp.pad(k_full, ((0, 0), (0, pad), (0, 0), (0, 0)))
    v_pad = jnp.pad(v_full, ((0, 0), (0, pad), (0, 0), (0, 0)))
    k_blk = k_pad.reshape(B, nb, MOBA_BLOCK, H, d).transpose(0, 3, 1, 2, 4)
    v_blk = v_pad.reshape(B, nb, MOBA_BLOCK, H, d).transpose(0, 3, 1, 2, 4)
    k_mean = jnp.mean(k_blk.astype(jnp.float32), axis=3)
    topk = min(MOBA_TOPK, nb)
    qb = _query_block(B, T)
    n_sub = T // qb
    q_sub = q.reshape(B, n_sub, qb, H, d).transpose(1, 0, 2, 3, 4)
    starts = q_pos0 + jnp.arange(n_sub, dtype=jnp.int32) * qb
    bias_t = rel_bias.T.astype(jnp.float32)
    b_ix = jnp.arange(B)[:, None, None, None]
    h_ix = jnp.arange(H)[None, None, :, None]
    offs = jnp.arange(MOBA_BLOCK, dtype=jnp.int32)
    blk_ids = jnp.arange(nb, dtype=jnp.int32)
    scale = d ** -0.5

    def one_block(args):
        qs, start = args
        qpos = start + jnp.arange(qb, dtype=jnp.int32)
        own = start // MOBA_BLOCK
        qf = qs.astype(jnp.float32)
        gate = jnp.einsum('bqhd,bhnd->bqhn', qf, k_mean)
        gate = jnp.where(blk_ids < own, gate, -jnp.inf)
        _, idx = lax.top_k(gate, topk)
        valid = idx < own
        k_sel = k_blk[b_ix, h_ix, idx].astype(jnp.float32)
        v_sel = v_blk[b_ix, h_ix, idx].astype(jnp.float32)
        kpos_sel = idx[..., None] * MOBA_BLOCK + offs
        dist_sel = qpos[None, :, None, None, None] - kpos_sel
        s_sel = jnp.einsum('bqhd,bqhkld->bqhkl', qf, k_sel) * scale
        s_sel = s_sel + bias_t[h_ix[..., None], t5_bucket(dist_sel)]
        s_sel = jnp.where(valid[..., None], s_sel, -jnp.inf)
        k_own = lax.dynamic_slice_in_dim(k_pad, own * MOBA_BLOCK, MOBA_BLOCK, axis=1).astype(jnp.float32)
        v_own = lax.dynamic_slice_in_dim(v_pad, own * MOBA_BLOCK, MOBA_BLOCK, axis=1).astype(jnp.float32)
        dist_own = qpos[:, None] - (own * MOBA_BLOCK + offs)[None, :]
        s_own = jnp.einsum('bqhd,blhd->bqhl', qf, k_own) * scale
        s_own = s_own + bias_t[:, t5_bucket(dist_own)].transpose(1, 0, 2)[None]
        s_own = jnp.where((dist_own >= 0)[None, :, None, :], s_own, -jnp.inf)
        logits = jnp.concatenate([s_sel.reshape(B, qb, H, topk * MOBA_BLOCK), s_own], axis=-1)
        p = jax.nn.softmax(logits, axis=-1)
        p_sel = p[..., :topk * MOBA_BLOCK].reshape(B, qb, H, topk, MOBA_BLOCK)
        p_own = p[..., topk * MOBA_BLOCK:]
        o = (jnp.einsum('bqhkl,bqhkld->bqhd', p_sel, v_sel)
             + jnp.einsum('bqhl,blhd->bqhd', p_own, v_own))
        return o.astype(qs.dtype)

    out = lax.map(one_block, (q_sub, starts))
    return out.transpose(1, 0, 2, 3, 4).reshape(B, T, H, d)


def mixer_b(h, w_in, g_q, g_k, w_out, rel_bias, k_past, v_past, q_pos0):
    B, T, _ = h.shape
    q, k, v, z = jnp.split(h @ w_in, 4, axis=-1)
    q = rms_norm(q.reshape(B, T, B_HEADS, HEAD_DIM), g_q)
    k = rms_norm(k.reshape(B, T, B_HEADS, HEAD_DIM), g_k)
    v = v.reshape(B, T, B_HEADS, HEAD_DIM)
    if k_past is None:
        k_full, v_full = k, v
    else:
        k_full = jnp.concatenate([k_past.astype(k.dtype), k], axis=1)
        v_full = jnp.concatenate([v_past.astype(v.dtype), v], axis=1)
    attn = moba_attention(q, k_full, v_full, q_pos0, rel_bias).reshape(B, T, E_B)
    y = (attn * jax.nn.silu(z)) @ w_out
    return y, k, v


def setup_inputs(seed: int = 0) -> dict:
    key = jax.random.key(seed)
    ks = jax.random.split(key, 20)
    f32 = jnp.float32
    n_pages = PAST_LEN // PAGE_SIZE
    n_used = DEC_BATCH * n_pages
    n_phys = n_used + n_used // 4
    perm = jax.random.permutation(ks[0], n_phys)
    page_table = perm[:n_used].reshape(DEC_BATCH, n_pages).astype(jnp.int32)
    x_prompt = jax.random.normal(ks[1], (BATCH, SEQ, D_MODEL), f32)
    x_sample = jax.random.normal(ks[2], (DEC_BATCH, DEC_SEQ, D_MODEL), f32)
    cache_k = jax.random.normal(ks[3], (N_B_LAYERS, n_phys, PAGE_SIZE, B_HEADS, HEAD_DIM), f32)
    cache_v = jax.random.normal(ks[4], (N_B_LAYERS, n_phys, PAGE_SIZE, B_HEADS, HEAD_DIM), f32)
    g_norm = 1.0 + 0.01 * jax.random.normal(ks[5], (DEPTH, D_MODEL), f32)
    rel_bias = 0.5 * jax.random.normal(ks[6], (N_BUCKETS, B_HEADS), f32)
    w_in_a = jax.random.normal(ks[7], (N_A_LAYERS, D_MODEL, 3 * E_A), f32) * D_MODEL ** -0.5
    g_v_a = 1.0 + 0.01 * jax.random.normal(ks[8], (N_A_LAYERS, E_A), f32)
    w_s_a = jax.random.normal(ks[9], (N_A_LAYERS, A_GROUPS, CHUNK, CHUNK), f32) * CHUNK ** -0.5
    b_s_a = 1.0 + 0.01 * jax.random.normal(ks[10], (N_A_LAYERS, A_GROUPS, CHUNK), f32)
    w_out_a = jax.random.normal(ks[11], (N_A_LAYERS, E_A, D_MODEL), f32) * E_A ** -0.5
    w_in_b = jax.random.normal(ks[12], (N_B_LAYERS, D_MODEL, 4 * E_B), f32) * D_MODEL ** -0.5
    g_q_b = 1.0 + 0.01 * jax.random.normal(ks[13], (N_B_LAYERS, HEAD_DIM), f32)
    g_k_b = 1.0 + 0.01 * jax.random.normal(ks[14], (N_B_LAYERS, HEAD_DIM), f32)
    w_out_b = jax.random.normal(ks[15], (N_B_LAYERS, E_B, D_MODEL), f32) * E_B ** -0.5
    return {'x_prompt': x_prompt, 'x_sample': x_sample, 'cache_k': cache_k, 'cache_v': cache_v,
            'page_table': page_table, 'g_norm': g_norm, 'rel_bias': rel_bias,
            'w_in_a': w_in_a, 'g_v_a': g_v_a, 'w_s_a': w_s_a, 'b_s_a': b_s_a, 'w_out_a': w_out_a,
            'w_in_b': w_in_b, 'g_q_b': g_q_b, 'g_k_b': g_k_b, 'w_out_b': w_out_b}


def reference(x_prompt, x_sample, cache_k, cache_v, page_table, g_norm, rel_bias,
              w_in_a, g_v_a, w_s_a, b_s_a, w_out_a, w_in_b, g_q_b, g_k_b, w_out_b):
    n_dec, n_pages = page_table.shape
    past_len = n_pages * PAGE_SIZE
    xp, xs = x_prompt, x_sample
    k_new_p, v_new_p, k_new_s, v_new_s, vch_s = [], [], [], [], []
    for i in range(DEPTH):
        j = i // N_MIXERS
        hp = rms_norm(xp, g_norm[i])
        hs = rms_norm(xs, g_norm[i])
        if i % N_MIXERS == 0:
            yp, _ = mixer_a(hp, w_in_a[j], g_v_a[j], w_s_a[j], b_s_a[j], w_out_a[j])
            ys, v_rows = mixer_a(hs, w_in_a[j], g_v_a[j], w_s_a[j], b_s_a[j], w_out_a[j])
            vch_s.append(v_rows)
        else:
            k_past = cache_k[j][page_table].reshape(n_dec, past_len, B_HEADS, HEAD_DIM)
            v_past = cache_v[j][page_table].reshape(n_dec, past_len, B_HEADS, HEAD_DIM)
            yp, kp, vp = mixer_b(hp, w_in_b[j], g_q_b[j], g_k_b[j], w_out_b[j], rel_bias, None, None, 0)
            ys, ks_, vs_ = mixer_b(hs, w_in_b[j], g_q_b[j], g_k_b[j], w_out_b[j], rel_bias, k_past, v_past, past_len)
            k_new_p.append(kp)
            v_new_p.append(vp)
            k_new_s.append(ks_)
            v_new_s.append(vs_)
        xp = xp + yp
        xs = xs + ys
    return (xp, xs, jnp.stack(k_new_p), jnp.stack(v_new_p), jnp.stack(k_new_s), jnp.stack(v_new_s), jnp.stack(vch_s))
```

```python
import functools
import math

import numpy as np
import jax
import jax.numpy as jnp
from jax import lax
from jax.experimental import pallas as pl
from jax.experimental.pallas import tpu as pltpu

EPS = 1e-6
CHUNK = 128
MOBA_BLOCK = 256
MOBA_TOPK = 3
MAX_DISTANCE = 128
NEG = -1e30
V7X_LANES = 128
V7X_VMEM_LIMIT_BYTES = 56 * 1024 * 1024
TOKEN_TILE = 256
OUT_TILE = 512
PAGES_PER_STEP = 4

bf16 = jnp.bfloat16
f32 = jnp.float32


def _dot(a, b):
    return jnp.dot(a, b, preferred_element_type=f32)


def _dot_nt(a, b):
    return lax.dot_general(a, b, (((1,), (1,)), ((), ())), preferred_element_type=f32)


def _gelu(x):
    return 0.5 * x * (1.0 + lax.erf(x * (2.0 ** -0.5)))


def _silu(x):
    return x * jax.nn.sigmoid(x)


def _rms_rows(x, g):
    return x * lax.rsqrt(jnp.mean(x * x, axis=-1, keepdims=True) + EPS) * g


def _params(n_axes):
    return pltpu.CompilerParams(dimension_semantics=("arbitrary",) * n_axes,
                                vmem_limit_bytes=V7X_VMEM_LIMIT_BYTES)


def _resident(shape):
    nd = len(shape)
    return pl.BlockSpec(shape, lambda *_: (0,) * nd, pipeline_mode=pl.Buffered(1))


def _bucket_breaks(n_buckets, max_n):
    n = np.arange(max_n + 1, dtype=np.int32)
    max_exact = n_buckets // 2
    nf = np.maximum(n, 1).astype(np.float32)
    large = max_exact + (np.log(nf / np.float32(max_exact)) / np.float32(math.log(MAX_DISTANCE / max_exact))
                         * np.float32(n_buckets - max_exact)).astype(np.int32)
    b = np.where(n < max_exact, n, np.minimum(large, n_buckets - 1))
    assert np.all(np.diff(b) >= 0)
    breaks = []
    for v in np.unique(b)[:-1]:
        breaks.append((int(v), int(np.max(n[b == v]))))
    top = int(b[-1])
    assert int(b[MOBA_BLOCK + 1]) == top
    return breaks, top


def _bias_of_dist(n, row_of_bucket, breaks, top):
    val = row_of_bucket(top)
    val = jnp.where(n <= breaks[-1][1], row_of_bucket(breaks[-1][0]), val)
    for bv, last in reversed(breaks[:-1]):
        val = jnp.where(n <= last, row_of_bucket(bv), val)
    return val


def _prompt_bias_kernel(rb_ref, o_ref, *, breaks, top):
    h = pl.program_id(0)
    r = lax.broadcasted_iota(jnp.int32, (MOBA_BLOCK, MOBA_BLOCK), 0)
    c = lax.broadcasted_iota(jnp.int32, (MOBA_BLOCK, MOBA_BLOCK), 1)
    for kind in (0, 1):
        n = r - c + kind * MOBA_BLOCK
        val = _bias_of_dist(n, lambda b: rb_ref[b, h], breaks, top)
        if kind == 0:
            val = jnp.where(n < 0, NEG, val)
        o_ref[0, kind] = val.astype(f32)


def _sample_bias_kernel(rbl_ref, last_ref, own_ref, *, breaks, top, dec_t):
    ht = rbl_ref.shape[1]
    t = lax.broadcasted_iota(jnp.int32, (1, ht), 1) & (dec_t - 1)
    row = lambda b: rbl_ref[b:b + 1, :]
    r = lax.broadcasted_iota(jnp.int32, last_ref.shape, 0)
    last_ref[...] = _bias_of_dist(MOBA_BLOCK + t - r, row, breaks, top)
    r = lax.broadcasted_iota(jnp.int32, own_ref.shape, 0)
    n = t - r
    own_ref[...] = jnp.where(n < 0, NEG, _bias_of_dist(n, row, breaks, top))


def _layer_a_kernel(x_ref, gn_ref, win_ref, gv_ref, wmix_ref, bmix_ref, wout_ref, y_ref, *rest,
                    sub_len, emit_v):
    if emit_v:
        v_ref, vn_scr = rest
    else:
        (vn_scr,) = rest
    tm = x_ref.shape[0]
    e = gv_ref.shape[1]
    n_groups = wmix_ref.shape[0]
    gd = e // n_groups
    n_chunks = tm // CHUNK
    wide = 2 * V7X_LANES
    g_per = wide // gd

    x = x_ref[...]
    h = _rms_rows(x, gn_ref[...]).astype(bf16)

    vg = _gelu(_dot(h, win_ref[:, e:2 * e]))
    vn = _rms_rows(vg, gv_ref[...])
    if emit_v:
        v_ref[...] = vn
    vn_scr[...] = vn.astype(bf16)

    r = lax.broadcasted_iota(jnp.int32, (CHUNK, CHUNK), 0)
    c = lax.broadcasted_iota(jnp.int32, (CHUNK, CHUNK), 1)
    keep = r >= c
    if sub_len != CHUNK:
        keep = keep & ((r ^ c) < sub_len)
    bias_rows = [bmix_ref[...]] * n_chunks

    acc = jnp.zeros(y_ref.shape, f32)
    for p in range(e // wide):
        lo = p * wide
        u = _gelu(_dot(h, win_ref[:, lo:lo + wide]))
        z = _dot(h, win_ref[:, 2 * e + lo:2 * e + lo + wide])
        parts = []
        for gg in range(g_per):
            g = p * g_per + gg
            wm = jnp.where(keep, wmix_ref[g], 0.0).astype(bf16)
            rhs = jnp.concatenate(
                [vn_scr[k * CHUNK:(k + 1) * CHUNK, g * gd:(g + 1) * gd] for k in range(n_chunks)], axis=1)
            sg = _dot(wm, rhs)
            parts.append(jnp.concatenate([sg[:, k * gd:(k + 1) * gd] for k in range(n_chunks)], axis=0))
        s = jnp.concatenate(parts, axis=1) + jnp.concatenate(
            [b[:, lo:lo + wide] for b in bias_rows], axis=0)
        t = (u * s * _silu(z)).astype(bf16)
        acc = acc + _dot(t, wout_ref[lo:lo + wide, :])
    y_ref[...] = x + acc


def _layer_a(x, gn, w_in, g_v, w_mix, b_mix, w_out, *, sub_len, emit_v):
    n, d = x.shape
    e = g_v.shape[1]
    tm = TOKEN_TILE
    assert n % tm == 0 and tm % CHUNK == 0 and e % (2 * V7X_LANES) == 0
    row_spec = lambda width: pl.BlockSpec((tm, width), lambda i: (i, 0))
    out_shape = [jax.ShapeDtypeStruct((n, d), f32)]
    out_specs = [row_spec(d)]
    if emit_v:
        out_shape.append(jax.ShapeDtypeStruct((n, e), f32))
        out_specs.append(row_spec(e))
    res = pl.pallas_call(
        functools.partial(_layer_a_kernel, sub_len=sub_len, emit_v=emit_v),
        grid=(n // tm,),
        in_specs=[row_spec(d), _resident(gn.shape), _resident(w_in.shape), _resident(g_v.shape),
                  _resident(w_mix.shape), _resident(b_mix.shape), _resident(w_out.shape)],
        out_specs=out_specs,
        out_shape=out_shape,
        scratch_shapes=[pltpu.VMEM((tm, e), bf16)],
        compiler_params=_params(1),
        name="layer_a_sample" if emit_v else "layer_a_prompt",
    )(x, gn, w_in, g_v, w_mix, b_mix, w_out)
    return res if emit_v else res[0]


def _proj_b_kernel(x_ref, gn_ref, win_ref, gq_ref, gk_ref, ind_ref, indt_ref, *outs, head_dim, sample):
    eb = gq_ref.shape[1]
    x = x_ref[...]
    h = _rms_rows(x, gn_ref[...]).astype(bf16)

    def head_norm(a, g_ref):
        ss = _dot((a * a).astype(bf16), ind_ref[...])
        rs = lax.rsqrt(ss * (1.0 / head_dim) + EPS)
        rs_hi = rs.astype(bf16)
        rs_lo = (rs - rs_hi.astype(f32)).astype(bf16)
        scale = _dot(rs_hi, indt_ref[...]) + _dot(rs_lo, indt_ref[...])
        return a * scale * g_ref[...]

    q = head_norm(_dot(h, win_ref[:, 0:eb]), gq_ref) * (head_dim ** -0.5)
    k = head_norm(_dot(h, win_ref[:, eb:2 * eb]), gk_ref)
    v = _dot(h, win_ref[:, 2 * eb:3 * eb])
    sz = _silu(_dot(h, win_ref[:, 3 * eb:4 * eb])).astype(bf16)
    if sample:
        q_ref, kn_ref, v_ref, sz_ref = outs
        q_ref[...] = q
    else:
        q_ref, kn_ref, v_ref, sz_ref, kb_ref, vb_ref, km_ref = outs
        q_ref[...] = q.astype(bf16)
        kb_ref[...] = k.astype(bf16)
        vb_ref[...] = v.astype(bf16)
        km_ref[0] = jnp.mean(k, axis=0, keepdims=True)
    kn_ref[...] = k
    v_ref[...] = v
    sz_ref[...] = sz


def _proj_b(x, gn, w_in, g_q, g_k, ind, indt, *, head_dim, sample):
    n, d = x.shape
    eb = g_q.shape[1]
    tm = TOKEN_TILE
    assert n % tm == 0 and tm == MOBA_BLOCK
    row = lambda: pl.BlockSpec((tm, eb), lambda i: (i, 0))
    sds = lambda dt: jax.ShapeDtypeStruct((n, eb), dt)
    if sample:
        out_shape = [sds(f32), sds(f32), sds(f32), sds(bf16)]
        out_specs = [row(), row(), row(), row()]
    else:
        out_shape = [sds(bf16), sds(f32), sds(f32), sds(bf16), sds(bf16), sds(bf16),
                     jax.ShapeDtypeStruct((n // tm, 1, eb), f32)]
        out_specs = [row() for _ in range(6)] + [pl.BlockSpec((1, 1, eb), lambda i: (i, 0, 0))]
    return pl.pallas_call(
        functools.partial(_proj_b_kernel, head_dim=head_dim, sample=sample),
        grid=(n // tm,),
        in_specs=[pl.BlockSpec((tm, d), lambda i: (i, 0)), _resident(gn.shape), _resident(w_in.shape),
                  _resident(g_q.shape), _resident(g_k.shape), _resident(ind.shape), _resident(indt.shape)],
        out_specs=out_specs,
        out_shape=out_shape,
        compiler_params=_params(1),
        name="proj_b_sample" if sample else "proj_b_prompt",
    )(x, gn, w_in, g_q, g_k, ind, indt)


def _attn_prompt_kernel(rb_ref, q_ref, k_ref, v_ref, km_ref, bias_ref, sz_ref, o_ref, *, head_dim, top):
    hp = pl.program_id(1)
    i = pl.program_id(2)
    blk = MOBA_BLOCK
    nb = km_ref.shape[0]
    heads_per = V7X_LANES // head_dim
    q = q_ref[...]
    lane = lax.broadcasted_iota(jnp.int32, (1, V7X_LANES), 1)
    bid = lax.broadcasted_iota(jnp.int32, (1, nb), 1)
    outs = []
    for hh in range(heads_per):
        in_head = (lane >= hh * head_dim) & (lane < (hh + 1) * head_dim)
        qm = jnp.where(in_head, q, jnp.zeros_like(q))
        km = jnp.where(in_head, km_ref[...], 0.0)
        km_hi = km.astype(bf16)
        km_lo = (km - km_hi.astype(f32)).astype(bf16)
        gate = _dot_nt(qm, km_hi) + _dot_nt(qm, km_lo)
        rank = jnp.zeros((blk, nb), jnp.int32)
        for jp in range(nb):
            col = gate[:, jp:jp + 1]
            beats = (col > gate) | ((col == gate) & (jp < bid))
            rank = rank + jnp.where(beats & (jp < i), 1, 0)
        sel = jnp.where((bid < i) & (rank < MOBA_TOPK), 1.0, 0.0)
        c_far = rb_ref[top, hp * heads_per + hh]

        def step(j, carry, kind, sel=sel, qm=qm, c_far=c_far, hh=hh):
            m, l, acc = carry
            off = pl.multiple_of(j * blk, blk)
            s = _dot_nt(qm, k_ref[pl.ds(off, blk), :])
            if kind == "own":
                s = s + bias_ref[hh, 0]
                c = 0.0
            elif kind == "prev":
                s = s + bias_ref[hh, 1]
                c = 0.0
            else:
                c = c_far
            mb = jnp.max(s, axis=-1, keepdims=True) + c
            if kind != "own":
                chosen = jnp.max(jnp.where(bid == j, sel, 0.0), axis=-1, keepdims=True) > 0.5
                mb = jnp.where(chosen, mb, NEG)
            m_new = jnp.maximum(m, mb)
            alpha = jnp.exp(m - m_new)
            shift = m_new - c
            if kind != "own":
                shift = jnp.where(chosen, shift, -NEG)
            p = jnp.exp(s - shift)
            l = alpha * l + jnp.sum(p, axis=-1, keepdims=True)
            acc = alpha * acc + _dot(p.astype(bf16), v_ref[pl.ds(off, blk), :])
            return m_new, l, acc

        carry = (jnp.full((blk, 1), NEG, f32), jnp.zeros((blk, 1), f32), jnp.zeros((blk, V7X_LANES), f32))
        carry = lax.fori_loop(0, i - 1, functools.partial(step, kind="far"), carry)
        carry = step(jnp.maximum(i - 1, 0), carry, "prev")
        m, l, acc = step(i, carry, "own")
        outs.append(acc / l)
    o = outs[-1]
    for hh in range(heads_per - 2, -1, -1):
        o = jnp.where(lane < (hh + 1) * head_dim, outs[hh], o)
    o_ref[...] = (o * sz_ref[...].astype(f32)).astype(bf16)


def _attn_prompt(rel_bias, q, kb, vb, kmean, bias_tiles, sz, *, batch, seq, head_dim, top):
    n, eb = q.shape
    blk = MOBA_BLOCK
    nb = seq // blk
    hw = V7X_LANES
    heads_per = hw // head_dim
    tile = lambda: pl.BlockSpec((blk, hw), lambda b, hp, i: (b * nb + i, hp))
    seqb = lambda: pl.BlockSpec((seq, hw), lambda b, hp, i: (b, hp))
    return pl.pallas_call(
        functools.partial(_attn_prompt_kernel, head_dim=head_dim, top=top),
        grid=(batch, eb // hw, nb),
        in_specs=[pl.BlockSpec(memory_space=pltpu.SMEM), tile(), seqb(), seqb(),
                  pl.BlockSpec((None, nb, hw), lambda b, hp, i: (b, 0, hp)),
                  pl.BlockSpec((heads_per, 2, blk, blk), lambda b, hp, i: (hp, 0, 0, 0)),
                  tile()],
        out_specs=tile(),
        out_shape=jax.ShapeDtypeStruct((n, eb), bf16),
        compiler_params=_params(3),
        name="attn_prompt",
    )(rel_bias, q, kb, vb, kmean, bias_tiles, sz)


def _attn_sample_kernel(pt_ref, q_ref, kn_ref, vn_ref, sz_ref, rbl_ref, lastb_ref, ownb_ref, *rest,
                        n_per, n_steps, head_dim, dec_t, top):
    del pt_ref
    k_refs = rest[:n_per]
    v_refs = rest[n_per:2 * n_per]
    o_ref, qbd_scr, qbdb_scr, s_scr, ksum_scr, acc_scr = rest[2 * n_per:]
    ps, eb = k_refs[0].shape
    ht = qbd_scr.shape[0]
    n_heads = eb // head_dim
    s_past = s_scr.shape[0] - ps
    nblk = s_past // MOBA_BLOCK
    ppb = MOBA_BLOCK // ps
    step = pl.program_id(1)
    lane_head = lax.broadcasted_iota(jnp.int32, (1, eb), 1) >> (head_dim.bit_length() - 1)

    @pl.when(step == 0)
    def _():
        qt = jnp.concatenate([q_ref[0]] * n_heads, axis=0)
        row_head = lax.broadcasted_iota(jnp.int32, (ht, 1), 0) >> (dec_t.bit_length() - 1)
        qbd = jnp.where(row_head == lane_head, qt, 0.0)
        qbd_scr[...] = qbd
        qbdb_scr[...] = qbd.astype(bf16)

    @pl.when(step < n_steps)
    def _():
        for g in range(n_per):
            page = step * n_per + g
            kp = k_refs[g][...]
            ksum_scr[pl.ds(page, 1), :] = jnp.sum(kp, axis=0, keepdims=True)
            s_scr[pl.ds(pl.multiple_of(page * ps, ps), ps), :] = _dot_nt(kp.astype(bf16), qbdb_scr[...])

    @pl.when(step == n_steps - 1)
    def _():
        ksum = ksum_scr[...]
        km = jnp.concatenate(
            [sum(ksum[j * ppb + pp:j * ppb + pp + 1, :] for pp in range(ppb)) for j in range(nblk)],
            axis=0) * (1.0 / MOBA_BLOCK)
        km_hi = km.astype(bf16)
        km_lo = (km - km_hi.astype(f32)).astype(bf16)
        q_hi = qbdb_scr[...]
        q_lo = (qbd_scr[...] - q_hi.astype(f32)).astype(bf16)
        gate = _dot_nt(km_hi, q_hi) + _dot_nt(km_hi, q_lo) + _dot_nt(km_lo, q_hi)
        bid = lax.broadcasted_iota(jnp.int32, (nblk, 1), 0)
        rank = jnp.zeros((nblk, ht), jnp.int32)
        for jp in range(nblk):
            rowv = gate[jp:jp + 1, :]
            beats = (rowv > gate) | ((rowv == gate) & (jp < bid))
            rank = rank + jnp.where(beats, 1, 0)
        sel = rank < MOBA_TOPK

        kn_pad = jnp.concatenate([kn_ref[0], jnp.zeros((ps - dec_t, eb), f32)], axis=0).astype(bf16)
        s_own = _dot_nt(kn_pad, qbdb_scr[...]) + ownb_ref[...]
        mx = jnp.max(s_own, axis=0, keepdims=True)
        c_far = rbl_ref[top:top + 1, :]
        for j in range(nblk):
            sj = s_scr[j * MOBA_BLOCK:(j + 1) * MOBA_BLOCK, :]
            sj = sj + (lastb_ref[...] if j == nblk - 1 else c_far)
            sj = jnp.where(sel[j:j + 1, :], sj, NEG)
            s_scr[j * MOBA_BLOCK:(j + 1) * MOBA_BLOCK, :] = sj
            mx = jnp.maximum(mx, jnp.max(sj, axis=0, keepdims=True))
        p_own = jnp.exp(s_own - mx)
        l = jnp.sum(p_own, axis=0, keepdims=True)
        for j in range(nblk):
            pj = jnp.exp(s_scr[j * MOBA_BLOCK:(j + 1) * MOBA_BLOCK, :] - mx)
            s_scr[j * MOBA_BLOCK:(j + 1) * MOBA_BLOCK, :] = pj
            l = l + jnp.sum(pj, axis=0, keepdims=True)
        inv = 1.0 / l
        for j in range(nblk):
            s_scr[j * MOBA_BLOCK:(j + 1) * MOBA_BLOCK, :] = s_scr[j * MOBA_BLOCK:(j + 1) * MOBA_BLOCK, :] * inv
        vn_pad = jnp.concatenate([vn_ref[0], jnp.zeros((ps - dec_t, eb), f32)], axis=0).astype(bf16)
        acc_scr[...] = _dot((p_own * inv).T.astype(bf16), vn_pad)

    @pl.when(step >= n_steps)
    def _():
        acc = acc_scr[...]
        for g in range(n_per):
            page = (step - n_steps) * n_per + g
            pp = s_scr[pl.ds(pl.multiple_of(page * ps, ps), ps), :]
            acc = acc + _dot(pp.T.astype(bf16), v_refs[g][...].astype(bf16))
        acc_scr[...] = acc

    @pl.when(step == 2 * n_steps - 1)
    def _():
        o = acc_scr[...]
        attn = jnp.zeros((dec_t, eb), f32)
        for hh in range(n_heads):
            attn = attn + jnp.where(lane_head == hh, o[hh * dec_t:(hh + 1) * dec_t, :], 0.0)
        o_ref[0] = (attn * sz_ref[0].astype(f32)).astype(bf16)


def _attn_sample(page_table, q, kn, vn, sz, rbl, lastb, ownb, cache_k, cache_v, *, layer, head_dim, top):
    nb, dec_t, eb = q.shape
    n_pages = page_table.shape[1]
    ps = cache_k.shape[2]
    n_per = PAGES_PER_STEP
    assert n_pages % n_per == 0 and MOBA_BLOCK % ps == 0 and (n_pages * ps) % MOBA_BLOCK == 0
    n_steps = n_pages // n_per
    ht = (eb // head_dim) * dec_t
    assert ht == V7X_LANES and dec_t & (dec_t - 1) == 0 and dec_t <= ps
    per_b = lambda dt: pl.BlockSpec((1, dec_t, eb), lambda b, s, pt: (b, 0, 0))
    whole = lambda a: pl.BlockSpec(a.shape, lambda b, s, pt: (0,) * a.ndim)

    def k_spec(g):
        return pl.BlockSpec((None, None, ps, eb),
                            lambda b, s, pt: (layer, pt[b, jnp.minimum(s, n_steps - 1) * n_per + g], 0, 0))

    def v_spec(g):
        return pl.BlockSpec((None, None, ps, eb),
                            lambda b, s, pt: (layer, pt[b, jnp.maximum(s - n_steps, 0) * n_per + g], 0, 0))

    grid_spec = pltpu.PrefetchScalarGridSpec(
        num_scalar_prefetch=1,
        grid=(nb, 2 * n_steps),
        in_specs=[per_b(f32), per_b(f32), per_b(f32), per_b(bf16), whole(rbl), whole(lastb), whole(ownb)]
                 + [k_spec(g) for g in range(n_per)] + [v_spec(g) for g in range(n_per)],
        out_specs=pl.BlockSpec((1, dec_t, eb), lambda b, s, pt: (b, 0, 0)),
        scratch_shapes=[pltpu.VMEM((ht, eb), f32), pltpu.VMEM((ht, eb), bf16),
                        pltpu.VMEM((n_pages * ps + ps, ht), f32), pltpu.VMEM((n_pages, eb), f32),
                        pltpu.VMEM((ht, eb), f32)],
    )
    return pl.pallas_call(
        functools.partial(_attn_sample_kernel, n_per=n_per, n_steps=n_steps, head_dim=head_dim,
                          dec_t=dec_t, top=top),
        grid_spec=grid_spec,
        out_shape=jax.ShapeDtypeStruct((nb, dec_t, eb), bf16),
        compiler_params=_params(2),
        name="attn_sample",
    )(page_table, q, kn, vn, sz, rbl, lastb, ownb, *([cache_k] * n_per), *([cache_v] * n_per))


def _out_proj_kernel(x_ref, t_ref, w_ref, y_ref):
    y_ref[...] = x_ref[...] + _dot(t_ref[...], w_ref[...])


def _out_proj(x, t, w):
    n, d = x.shape
    tm = min(OUT_TILE, n)
    assert n % tm == 0
    return pl.pallas_call(
        _out_proj_kernel,
        grid=(n // tm,),
        in_specs=[pl.BlockSpec((tm, d), lambda i: (i, 0)), pl.BlockSpec((tm, t.shape[1]), lambda i: (i, 0)),
                  _resident(w.shape)],
        out_specs=pl.BlockSpec((tm, d), lambda i: (i, 0)),
        out_shape=jax.ShapeDtypeStruct((n, d), f32),
        compiler_params=_params(1),
        name="out_proj",
    )(x, t, w)


def kernel(x_prompt, x_sample, cache_k, cache_v, page_table, g_norm, rel_bias, w_in_a, g_v_a, w_s_a, b_s_a,
           w_out_a, w_in_b, g_q_b, g_k_b, w_out_b):
    batch, seq, d = x_prompt.shape
    n_dec, dec_t, _ = x_sample.shape
    depth = g_norm.shape[0]
    n_buckets, n_heads = rel_bias.shape
    head_dim = g_q_b.shape[1]
    eb = n_heads * head_dim
    e_a = g_v_a.shape[1]
    n_groups = w_s_a.shape[1]
    n_phys, page_size = cache_k.shape[1], cache_k.shape[2]
    past_len = page_table.shape[1] * page_size
    assert seq % MOBA_BLOCK == 0 and past_len % MOBA_BLOCK == 0 and CHUNK % dec_t == 0
    assert head_dim & (head_dim - 1) == 0 and V7X_LANES % head_dim == 0

    breaks, top = _bucket_breaks(n_buckets, 2 * MOBA_BLOCK + dec_t)
    heads_per = V7X_LANES // head_dim
    bias_tiles = pl.pallas_call(
        functools.partial(_prompt_bias_kernel, breaks=breaks, top=top),
        grid=(n_heads,),
        in_specs=[pl.BlockSpec(memory_space=pltpu.SMEM)],
        out_specs=pl.BlockSpec((1, 2, MOBA_BLOCK, MOBA_BLOCK), lambda h: (h, 0, 0, 0)),
        out_shape=jax.ShapeDtypeStruct((n_heads, 2, MOBA_BLOCK, MOBA_BLOCK), f32),
        compiler_params=_params(1),
        name="prompt_bias",
    )(rel_bias)
    rbl = jnp.repeat(rel_bias, dec_t, axis=1)
    lastb, ownb = pl.pallas_call(
        functools.partial(_sample_bias_kernel, breaks=breaks, top=top, dec_t=dec_t),
        out_shape=[jax.ShapeDtypeStruct((MOBA_BLOCK, n_heads * dec_t), f32),
                   jax.ShapeDtypeStruct((page_size, n_heads * dec_t), f32)],
        name="sample_bias",
    )(rbl)

    head_of_lane = jnp.arange(eb, dtype=jnp.int32) // head_dim
    ind = (head_of_lane[:, None] == jnp.arange(V7X_LANES, dtype=jnp.int32)[None, :]).astype(bf16)
    indt = ind.T

    cache_k = cache_k.reshape(cache_k.shape[0], n_phys, page_size, eb)
    cache_v = cache_v.reshape(cache_v.shape[0], n_phys, page_size, eb)

    xp = x_prompt.reshape(batch * seq, d)
    xs = x_sample.reshape(n_dec * dec_t, d)
    k_p, v_p, k_s, v_s, vch = [], [], [], [], []
    for i in range(depth):
        j = i // 2
        gn = g_norm[i][None, :]
        if i % 2 == 0:
            w_in = w_in_a[j].astype(bf16)
            w_out = w_out_a[j].astype(bf16)
            gv = g_v_a[j][None, :]
            gd = e_a // n_groups
            b_prompt = jnp.repeat(b_s_a[j].T, gd, axis=1)
            b_sample = jnp.repeat(jnp.tile(b_s_a[j][:, :dec_t], (1, CHUNK // dec_t)).T, gd, axis=1)
            w_sample = jnp.tile(w_s_a[j][:, :dec_t, :dec_t], (1, CHUNK // dec_t, CHUNK // dec_t))
            xp = _layer_a(xp, gn, w_in, gv, w_s_a[j], b_prompt, w_out, sub_len=CHUNK, emit_v=False)
            xs, v_rows = _layer_a(xs, gn, w_in, gv, w_sample, b_sample, w_out, sub_len=dec_t, emit_v=True)
            vch.append(v_rows.reshape(n_dec, dec_t, e_a))
        else:
            w_in = w_in_b[j].astype(bf16)
            w_out = w_out_b[j].astype(bf16)
            gq = jnp.tile(g_q_b[j], n_heads)[None, :]
            gk = jnp.tile(g_k_b[j], n_heads)[None, :]
            qp, knp, vnp, szp, kbp, vbp, kmp = _proj_b(xp, gn, w_in, gq, gk, ind, indt,
                                                       head_dim=head_dim, sample=False)
            qs, kns, vns, szs = _proj_b(xs, gn, w_in, gq, gk, ind, indt, head_dim=head_dim, sample=True)
            tp = _attn_prompt(rel_bias, qp, kbp, vbp, kmp.reshape(batch, seq // MOBA_BLOCK, eb), bias_tiles,
                              szp, batch=batch, seq=seq, head_dim=head_dim, top=top)
            r3 = lambda a: a.reshape(n_dec, dec_t, eb)
            ts = _attn_sample(page_table, r3(qs), r3(kns), r3(vns), r3(szs), rbl, lastb, ownb,
                              cache_k, cache_v, layer=j, head_dim=head_dim, top=top)
            xp = _out_proj(xp, tp, w_out)
            xs = _out_proj(xs, ts.reshape(n_dec * dec_t, eb), w_out)
            k_p.append(knp.reshape(batch, seq, n_heads, head_dim))
            v_p.append(vnp.reshape(batch, seq, n_heads, head_dim))
            k_s.append(kns.reshape(n_dec, dec_t, n_heads, head_dim))
            v_s.append(vns.reshape(n_dec, dec_t, n_heads, head_dim))
    return (xp.reshape(batch, seq, d), xs.reshape(n_dec, dec_t, d), jnp.stack(k_p), jnp.stack(v_p),
            jnp.stack(k_s), jnp.stack(v_s), jnp.stack(vch))
```

```python
import functools
import math

import numpy as np
import jax
import jax.numpy as jnp
from jax import lax
from jax.experimental import pallas as pl
from jax.experimental.pallas import tpu as pltpu

EPS = 1e-6
CHUNK = 128
MOBA_BLOCK = 256
MOBA_TOPK = 3
MAX_DISTANCE = 128
NEG = -1e30
V7X_LANES = 128
V7X_VMEM_LIMIT_BYTES = 56 * 1024 * 1024
TOKEN_TILE = 256
OUT_TILE = 512
PAGES_PER_STEP = 4

bf16 = jnp.bfloat16
f32 = jnp.float32


def _dot(a, b):
    return jnp.dot(a, b, preferred_element_type=f32)


def _dot_nt(a, b):
    return lax.dot_general(a, b, (((1,), (1,)), ((), ())), preferred_element_type=f32)


def _gelu(x):
    return 0.5 * x * (1.0 + lax.erf(x * (2.0 ** -0.5)))


def _silu(x):
    return x * jax.nn.sigmoid(x)


def _rms_rows(x, g):
    return x * lax.rsqrt(jnp.mean(x * x, axis=-1, keepdims=True) + EPS) * g


def _params(n_axes):
    return pltpu.CompilerParams(dimension_semantics=("arbitrary",) * n_axes,
                                vmem_limit_bytes=V7X_VMEM_LIMIT_BYTES)


def _resident(shape):
    nd = len(shape)
    return pl.BlockSpec(shape, lambda *_: (0,) * nd, pipeline_mode=pl.Buffered(1))


def _rank_rows(gate, n_valid):
    nb = gate.shape[0]
    bid = lax.broadcasted_iota(jnp.int32, (nb, 1), 0)
    rank = jnp.zeros(gate.shape, jnp.int32)
    for jp in range(nb):
        rowv = gate[jp:jp + 1, :]
        beats = (rowv > gate) | ((rowv == gate) & (jp < bid))
        rank = rank + jnp.where(beats & (jp < n_valid), 1, 0)
    return rank, bid


def _bucket_breaks(n_buckets, max_n):
    n = np.arange(max_n + 1, dtype=np.int32)
    max_exact = n_buckets // 2
    nf = np.maximum(n, 1).astype(np.float32)
    large = max_exact + (np.log(nf / np.float32(max_exact)) / np.float32(math.log(MAX_DISTANCE / max_exact))
                         * np.float32(n_buckets - max_exact)).astype(np.int32)
    b = np.where(n < max_exact, n, np.minimum(large, n_buckets - 1))
    assert np.all(np.diff(b) >= 0)
    breaks = []
    for v in np.unique(b)[:-1]:
        breaks.append((int(v), int(np.max(n[b == v]))))
    top = int(b[-1])
    assert int(b[MOBA_BLOCK + 1]) == top
    return breaks, top


def _bias_of_dist(n, value_of_bucket, breaks, top):
    val = jnp.where(n <= breaks[-1][1], value_of_bucket(breaks[-1][0]), value_of_bucket(top))
    for bv, last in reversed(breaks[:-1]):
        val = jnp.where(n <= last, value_of_bucket(bv), val)
    return val


def _prompt_bias_kernel(rb_ref, o_ref, *, breaks, top):
    h = pl.program_id(0)
    key = lax.broadcasted_iota(jnp.int32, (MOBA_BLOCK, MOBA_BLOCK), 0)
    qry = lax.broadcasted_iota(jnp.int32, (MOBA_BLOCK, MOBA_BLOCK), 1)
    for kind in (0, 1):
        n = qry - key + kind * MOBA_BLOCK
        val = _bias_of_dist(n, lambda b: rb_ref[b, h], breaks, top)
        if kind == 0:
            val = jnp.where(n < 0, NEG, val)
        o_ref[0, kind] = val.astype(f32)


def _sample_bias_kernel(rbt_ref, last_ref, own_ref, far_ref, *, breaks, top, dec_t):
    ht = rbt_ref.shape[0]
    t = lax.broadcasted_iota(jnp.int32, (ht, 1), 0) & (dec_t - 1)
    col = lambda b: rbt_ref[:, b:b + 1]
    c = lax.broadcasted_iota(jnp.int32, (1, last_ref.shape[1]), 1)
    last_ref[...] = _bias_of_dist(MOBA_BLOCK + t - c, col, breaks, top)
    c = lax.broadcasted_iota(jnp.int32, (1, own_ref.shape[1]), 1)
    n = t - c
    own_ref[...] = jnp.where(n < 0, NEG, _bias_of_dist(n, col, breaks, top))
    far_ref[...] = jnp.broadcast_to(col(top), far_ref.shape)


def _layer_a_kernel(x_ref, gn_ref, win_ref, gv_ref, wmix_ref, bmix_ref, wout_ref, y_ref, *rest,
                    sub_len, emit_v):
    if emit_v:
        v_ref, vn_scr = rest
    else:
        (vn_scr,) = rest
    tm = x_ref.shape[0]
    e = gv_ref.shape[1]
    n_groups = wmix_ref.shape[0]
    gd = e // n_groups
    n_chunks = tm // CHUNK
    wide = 2 * V7X_LANES
    g_per = wide // gd

    x = x_ref[...]
    h = _rms_rows(x, gn_ref[...]).astype(bf16)

    vg = _gelu(_dot(h, win_ref[:, e:2 * e]))
    vn = _rms_rows(vg, gv_ref[...])
    if emit_v:
        v_ref[...] = vn
    vn_scr[...] = vn.astype(bf16)

    r = lax.broadcasted_iota(jnp.int32, (CHUNK, CHUNK), 0)
    c = lax.broadcasted_iota(jnp.int32, (CHUNK, CHUNK), 1)
    keep = r >= c
    if sub_len != CHUNK:
        keep = keep & ((r ^ c) < sub_len)
    bias_rows = [bmix_ref[...]] * n_chunks

    acc = jnp.zeros(y_ref.shape, f32)
    for p in range(e // wide):
        lo = p * wide
        u = _gelu(_dot(h, win_ref[:, lo:lo + wide]))
        z = _dot(h, win_ref[:, 2 * e + lo:2 * e + lo + wide])
        parts = []
        for gg in range(g_per):
            g = p * g_per + gg
            wm = jnp.where(keep, wmix_ref[g], 0.0).astype(bf16)
            rhs = jnp.concatenate(
                [vn_scr[k * CHUNK:(k + 1) * CHUNK, g * gd:(g + 1) * gd] for k in range(n_chunks)], axis=1)
            sg = _dot(wm, rhs)
            parts.append(jnp.concatenate([sg[:, k * gd:(k + 1) * gd] for k in range(n_chunks)], axis=0))
        s = jnp.concatenate(parts, axis=1) + jnp.concatenate(
            [b[:, lo:lo + wide] for b in bias_rows], axis=0)
        t = (u * s * _silu(z)).astype(bf16)
        acc = acc + _dot(t, wout_ref[lo:lo + wide, :])
    y_ref[...] = x + acc


def _layer_a(x, gn, w_in, g_v, w_mix, b_mix, w_out, *, sub_len, emit_v):
    n, d = x.shape
    e = g_v.shape[1]
    tm = TOKEN_TILE
    assert n % tm == 0 and tm % CHUNK == 0 and e % (2 * V7X_LANES) == 0
    row_spec = lambda width: pl.BlockSpec((tm, width), lambda i: (i, 0))
    out_shape = [jax.ShapeDtypeStruct((n, d), f32)]
    out_specs = [row_spec(d)]
    if emit_v:
        out_shape.append(jax.ShapeDtypeStruct((n, e), f32))
        out_specs.append(row_spec(e))
    res = pl.pallas_call(
        functools.partial(_layer_a_kernel, sub_len=sub_len, emit_v=emit_v),
        grid=(n // tm,),
        in_specs=[row_spec(d), _resident(gn.shape), _resident(w_in.shape), _resident(g_v.shape),
                  _resident(w_mix.shape), _resident(b_mix.shape), _resident(w_out.shape)],
        out_specs=out_specs,
        out_shape=out_shape,
        scratch_shapes=[pltpu.VMEM((tm, e), bf16)],
        compiler_params=_params(1),
        name="layer_a_sample" if emit_v else "layer_a_prompt",
    )(x, gn, w_in, g_v, w_mix, b_mix, w_out)
    return res if emit_v else res[0]


def _head_norm_t(a, g_ref, head_dim):
    eb, tm = a.shape
    a3 = a.reshape(eb // head_dim, head_dim, tm)
    rs = lax.rsqrt(jnp.mean(a3 * a3, axis=1, keepdims=True) + EPS)
    g = jnp.concatenate([g_ref[...]] * (tm // V7X_LANES), axis=1).reshape(a3.shape)
    return (a3 * rs * g).reshape(eb, tm)


def _head_norm_rows(a, g_ref, ind_ref, indt_ref, head_dim):
    ss = _dot((a * a).astype(bf16), ind_ref[...])
    rs = lax.rsqrt(ss * (1.0 / head_dim) + EPS)
    rs_hi = rs.astype(bf16)
    rs_lo = (rs - rs_hi.astype(f32)).astype(bf16)
    scale = _dot(rs_hi, indt_ref[...]) + _dot(rs_lo, indt_ref[...])
    return a * scale * g_ref[...]


def _proj_b_prompt_kernel(x_ref, gn_ref, wt_ref, wz_ref, gq_ref, gk_ref,
                          kt_ref, vt_ref, qtb_ref, vtb_ref, ktok_ref, sz_ref, *, head_dim):
    eb = wz_ref.shape[1]
    h = _rms_rows(x_ref[...], gn_ref[...]).astype(bf16)
    qt = _head_norm_t(_dot_nt(wt_ref[0:eb, :], h), gq_ref, head_dim) * (head_dim ** -0.5)
    qtb_ref[...] = qt.astype(bf16)
    kt = _head_norm_t(_dot_nt(wt_ref[eb:2 * eb, :], h), gk_ref, head_dim)
    kt_ref[...] = kt
    ktok_ref[...] = kt.T.astype(bf16)
    vt = _dot_nt(wt_ref[2 * eb:3 * eb, :], h)
    vt_ref[...] = vt
    vtb_ref[...] = vt.astype(bf16)
    sz_ref[...] = _silu(_dot(h, wz_ref[...])).astype(bf16)


def _proj_b_prompt(x, gn, w_t, w_z, gq_t, gk_t, *, batch, seq, head_dim):
    n, d = x.shape
    eb = w_z.shape[1]
    tm = TOKEN_TILE
    assert seq % tm == 0 and tm == MOBA_BLOCK
    nt = seq // tm
    final_t = lambda: pl.BlockSpec((None, eb, tm), lambda i: (i // nt, 0, i % nt))
    tile_t = lambda: pl.BlockSpec((None, eb, tm), lambda i: (i, 0, 0))
    rows = lambda: pl.BlockSpec((tm, eb), lambda i: (i, 0))
    return pl.pallas_call(
        functools.partial(_proj_b_prompt_kernel, head_dim=head_dim),
        grid=(n // tm,),
        in_specs=[pl.BlockSpec((tm, d), lambda i: (i, 0)), _resident(gn.shape), _resident(w_t.shape),
                  _resident(w_z.shape), _resident(gq_t.shape), _resident(gk_t.shape)],
        out_specs=[final_t(), final_t(), tile_t(), tile_t(), rows(), rows()],
        out_shape=[jax.ShapeDtypeStruct((batch, eb, seq), f32), jax.ShapeDtypeStruct((batch, eb, seq), f32),
                   jax.ShapeDtypeStruct((n // tm, eb, tm), bf16), jax.ShapeDtypeStruct((n // tm, eb, tm), bf16),
                   jax.ShapeDtypeStruct((n, eb), bf16), jax.ShapeDtypeStruct((n, eb), bf16)],
        compiler_params=_params(1),
        name="proj_b_prompt",
    )(x, gn, w_t, w_z, gq_t, gk_t)


def _proj_b_sample_kernel(x_ref, gn_ref, w_ref, wt_ref, gq_ref, gk_ref, gkt_ref, ind_ref, indt_ref,
                          q_ref, kn_ref, vn_ref, sz_ref, kt_ref, vt_ref, *, head_dim):
    nseq = x_ref.shape[0]
    eb = gq_ref.shape[1]
    d = gn_ref.shape[1]
    t_per = x_ref.shape[1] // d
    x = jnp.concatenate([x_ref[:, t * d:(t + 1) * d] for t in range(t_per)], axis=0)
    h = _rms_rows(x, gn_ref[...]).astype(bf16)

    def put_rows(ref, a):
        for t in range(t_per):
            ref[:, t * eb:(t + 1) * eb] = a[t * nseq:(t + 1) * nseq].astype(ref.dtype)

    def put_t(ref, a):
        for t in range(t_per):
            ref[t] = a[:, t * nseq:(t + 1) * nseq]

    q = _head_norm_rows(_dot(h, w_ref[:, 0:eb]), gq_ref, ind_ref, indt_ref, head_dim) * (head_dim ** -0.5)
    put_rows(q_ref, q)
    put_rows(kn_ref, _head_norm_rows(_dot(h, w_ref[:, eb:2 * eb]), gk_ref, ind_ref, indt_ref, head_dim))
    put_rows(vn_ref, _dot(h, w_ref[:, 2 * eb:3 * eb]))
    put_rows(sz_ref, _silu(_dot(h, w_ref[:, 3 * eb:4 * eb])))
    put_t(kt_ref, _head_norm_t(_dot_nt(wt_ref[eb:2 * eb, :], h), gkt_ref, head_dim))
    put_t(vt_ref, _dot_nt(wt_ref[2 * eb:3 * eb, :], h))


def _proj_b_sample(x, gn, w, w_t, gq, gk, gk_t, ind, indt, *, dec_t, head_dim):
    nseq = x.shape[0]
    d = gn.shape[1]
    eb = gq.shape[1]
    assert nseq == V7X_LANES and TOKEN_TILE % nseq == 0
    t_per = TOKEN_TILE // nseq
    assert dec_t % t_per == 0
    rows = lambda: pl.BlockSpec((nseq, t_per * eb), lambda i: (0, i))
    sds = lambda dt: jax.ShapeDtypeStruct((nseq, dec_t * eb), dt)
    tsp = lambda: pl.BlockSpec((t_per, eb, nseq), lambda i: (i, 0, 0))
    tsd = jax.ShapeDtypeStruct((dec_t, eb, nseq), f32)
    return pl.pallas_call(
        functools.partial(_proj_b_sample_kernel, head_dim=head_dim),
        grid=(dec_t // t_per,),
        in_specs=[pl.BlockSpec((nseq, t_per * d), lambda i: (0, i)), _resident(gn.shape), _resident(w.shape),
                  _resident(w_t.shape), _resident(gq.shape), _resident(gk.shape), _resident(gk_t.shape),
                  _resident(ind.shape), _resident(indt.shape)],
        out_specs=[rows(), rows(), rows(), rows(), tsp(), tsp()],
        out_shape=[sds(f32), sds(f32), sds(f32), sds(bf16), tsd, tsd],
        compiler_params=_params(1),
        name="proj_b_sample",
    )(x, gn, w, w_t, gq, gk, gk_t, ind, indt)


def _attn_prompt_kernel(rb_ref, qt_ref, k_ref, vt_ref, bias_ref, sz_ref, o_ref, s_scr, *, head_dim, top):
    hp = pl.program_id(1)
    blk = MOBA_BLOCK
    nb, hw, _ = qt_ref.shape
    heads_per = hw // head_dim
    row = lax.broadcasted_iota(jnp.int32, (hw, 1), 0)
    for i in range(nb):
        qt = qt_ref[i]
        outs = []
        for hh in range(heads_per):
            in_head = (row >= hh * head_dim) & (row < (hh + 1) * head_dim)
            qm = jnp.where(in_head, qt, jnp.zeros_like(qt))
            c_far = rb_ref[top, hp * heads_per + hh]
            gates, maxes = [], []
            for j in range(i + 1):
                s = _dot(k_ref[j * blk:(j + 1) * blk, :], qm)
                if j < i:
                    gates.append(jnp.sum(s, axis=0, keepdims=True))
                if j == i:
                    s = s + bias_ref[hh, 0]
                elif j == i - 1:
                    s = s + bias_ref[hh, 1]
                s_scr[hh, j] = s
                mx = jnp.max(s, axis=0, keepdims=True)
                maxes.append(mx + c_far if j < i - 1 else mx)
            chosen = [None] * i
            if i > MOBA_TOPK:
                for j in range(i):
                    rank = jnp.zeros((1, blk), jnp.int32)
                    for jp in range(i):
                        if jp != j:
                            beats = gates[jp] >= gates[j] if jp < j else gates[jp] > gates[j]
                            rank = rank + jnp.where(beats, 1, 0)
                    chosen[j] = rank < MOBA_TOPK
            m = maxes[i]
            for j in range(i):
                m = jnp.maximum(m, maxes[j] if chosen[j] is None else jnp.where(chosen[j], maxes[j], NEG))
            l = jnp.zeros((1, blk), f32)
            acc = jnp.zeros((hw, blk), f32)
            for j in range(i + 1):
                shift = m - c_far if j < i - 1 else m
                if j < i and chosen[j] is not None:
                    shift = jnp.where(chosen[j], shift, -NEG)
                p = jnp.exp(s_scr[hh, j] - shift)
                l = l + jnp.sum(p, axis=0, keepdims=True)
                acc = acc + _dot(vt_ref[j], p.astype(bf16))
            outs.append(acc / l)
        ot = outs[-1]
        for hh in range(heads_per - 2, -1, -1):
            ot = jnp.where(row < (hh + 1) * head_dim, outs[hh], ot)
        rows = slice(i * blk, (i + 1) * blk)
        o_ref[rows, :] = (ot.T * sz_ref[rows, :].astype(f32)).astype(bf16)


def _attn_prompt(rel_bias, qtb, ktok, vtb, bias_tiles, sz, *, batch, seq, head_dim, top):
    n, eb = ktok.shape
    blk = MOBA_BLOCK
    nb = seq // blk
    hw = V7X_LANES
    heads_per = hw // head_dim
    by_block = lambda a: a.reshape(batch, nb, eb, blk)
    seq_rows = lambda: pl.BlockSpec((seq, hw), lambda b, hp: (b, hp))
    seq_t = lambda: pl.BlockSpec((None, nb, hw, blk), lambda b, hp: (b, 0, hp, 0))
    return pl.pallas_call(
        functools.partial(_attn_prompt_kernel, head_dim=head_dim, top=top),
        grid=(batch, eb // hw),
        in_specs=[pl.BlockSpec(memory_space=pltpu.SMEM), seq_t(), seq_rows(), seq_t(),
                  pl.BlockSpec((heads_per, 2, blk, blk), lambda b, hp: (hp, 0, 0, 0)), seq_rows()],
        out_specs=seq_rows(),
        out_shape=jax.ShapeDtypeStruct((n, eb), bf16),
        scratch_shapes=[pltpu.VMEM((heads_per, nb, blk, blk), f32)],
        compiler_params=_params(2),
        name="attn_prompt",
    )(rel_bias, by_block(qtb), ktok, by_block(vtb), bias_tiles, sz)


def _attn_sample_kernel(pt_ref, q_ref, kn_ref, vn_ref, sz_ref, far_ref, lastb_ref, ownb_ref, *rest,
                        n_per, n_steps, head_dim, dec_t):
    del pt_ref
    k_refs = rest[:n_per]
    v_refs = rest[n_per:2 * n_per]
    o_ref, qbd_scr, s_scr, acc_scr, inv_scr = rest[2 * n_per:]
    eb, ps = k_refs[0].shape
    ht = qbd_scr.shape[0]
    n_heads = eb // head_dim
    wide = n_per * ps
    nblk = n_steps * wide // MOBA_BLOCK
    bps = wide // MOBA_BLOCK
    step = pl.program_id(1)
    lane_head = lax.broadcasted_iota(jnp.int32, (1, eb), 1) >> (head_dim.bit_length() - 1)
    pad_rows = lambda a: jnp.concatenate([a, jnp.zeros((ps - dec_t, eb), f32)], axis=0).astype(bf16)

    @pl.when(step == 0)
    def _():
        qt = jnp.concatenate([q_ref[0]] * n_heads, axis=0)
        row_head = lax.broadcasted_iota(jnp.int32, (ht, 1), 0) >> (dec_t.bit_length() - 1)
        qbd_scr[...] = jnp.where(row_head == lane_head, qt, 0.0).astype(bf16)

    @pl.when(step < n_steps)
    def _():
        kcat = jnp.concatenate([k_refs[g][...].astype(bf16) for g in range(n_per)], axis=1)
        s_scr[step] = _dot(qbd_scr[...], kcat)

    @pl.when(step == n_steps - 1)
    def _():
        blocks = [(j // bps, (j % bps) * MOBA_BLOCK) for j in range(nblk)]
        lane = lax.broadcasted_iota(jnp.int32, (1, V7X_LANES), 1)
        gm = jnp.zeros((ht, V7X_LANES), f32)
        for j, (c, lo) in enumerate(blocks):
            gm = jnp.where(lane == j, jnp.sum(s_scr[c, :, lo:lo + MOBA_BLOCK], axis=1, keepdims=True), gm)
        nrow = -(-nblk // 8) * 8
        rank, bid = _rank_rows(gm.T[0:nrow, :], nblk)
        sel = jnp.where((bid < nblk) & (rank < MOBA_TOPK), 1.0, 0.0)
        selm = jnp.concatenate([sel, jnp.zeros((V7X_LANES - nrow, ht), f32)], axis=0).T

        s_own = _dot_nt(qbd_scr[...], pad_rows(kn_ref[0])) + ownb_ref[...]
        far = jnp.concatenate([far_ref[...]] * (MOBA_BLOCK // V7X_LANES), axis=1)
        mx = s_own
        for j, (c, lo) in enumerate(blocks):
            sj = s_scr[c, :, lo:lo + MOBA_BLOCK] + (lastb_ref[...] if j == nblk - 1 else far)
            sj = jnp.where(selm[:, j:j + 1] > 0.5, sj, NEG)
            s_scr[c, :, lo:lo + MOBA_BLOCK] = sj
            for q in range(MOBA_BLOCK // V7X_LANES):
                mx = jnp.maximum(mx, sj[:, q * V7X_LANES:(q + 1) * V7X_LANES])
        mx = jnp.max(mx, axis=1, keepdims=True)
        p_own = jnp.exp(s_own - mx)
        lsum = p_own
        for j, (c, lo) in enumerate(blocks):
            pj = jnp.exp(s_scr[c, :, lo:lo + MOBA_BLOCK] - mx)
            s_scr[c, :, lo:lo + MOBA_BLOCK] = pj
            for q in range(MOBA_BLOCK // V7X_LANES):
                lsum = lsum + pj[:, q * V7X_LANES:(q + 1) * V7X_LANES]
        inv_scr[...] = jnp.broadcast_to(1.0 / jnp.sum(lsum, axis=1, keepdims=True), inv_scr.shape)
        acc_scr[...] = _dot(p_own.astype(bf16), pad_rows(vn_ref[0]))

    @pl.when(step >= n_steps)
    def _():
        vcat = jnp.concatenate([v_refs[g][...].astype(bf16) for g in range(n_per)], axis=1)
        acc_scr[...] = acc_scr[...] + _dot_nt(s_scr[step - n_steps].astype(bf16), vcat)

    @pl.when(step == 2 * n_steps - 1)
    def _():
        inv = jnp.concatenate([inv_scr[...]] * (eb // V7X_LANES), axis=1)
        o = acc_scr[...] * inv
        attn = jnp.zeros((dec_t, eb), f32)
        for hh in range(n_heads):
            attn = attn + jnp.where(lane_head == hh, o[hh * dec_t:(hh + 1) * dec_t, :], 0.0)
        o_ref[0] = (attn * sz_ref[0].astype(f32)).astype(bf16)


def _attn_sample(page_table, q, kn, vn, sz, far, lastb, ownb, cache_k, cache_v, *, layer, head_dim):
    nb, dec_t, eb = q.shape
    n_pages = page_table.shape[1]
    ps = cache_k.shape[3]
    n_per = PAGES_PER_STEP
    wide = n_per * ps
    assert n_pages % n_per == 0 and wide % MOBA_BLOCK == 0 and ps == V7X_LANES
    n_steps = n_pages // n_per
    ht = (eb // head_dim) * dec_t
    assert ht == V7X_LANES and dec_t & (dec_t - 1) == 0 and dec_t <= ps and n_pages * ps // MOBA_BLOCK <= ht
    per_b = lambda: pl.BlockSpec((1, dec_t, eb), lambda b, s, pt: (b, 0, 0))
    whole = lambda a: pl.BlockSpec(a.shape, lambda b, s, pt: (0,) * a.ndim)

    def k_spec(g):
        return pl.BlockSpec((None, None, eb, ps),
                            lambda b, s, pt: (layer, pt[b, jnp.minimum(s, n_steps - 1) * n_per + g], 0, 0))

    def v_spec(g):
        return pl.BlockSpec((None, None, eb, ps),
                            lambda b, s, pt: (layer, pt[b, jnp.maximum(s - n_steps, 0) * n_per + g], 0, 0))

    grid_spec = pltpu.PrefetchScalarGridSpec(
        num_scalar_prefetch=1,
        grid=(nb, 2 * n_steps),
        in_specs=[per_b(), per_b(), per_b(), per_b(), whole(far), whole(lastb), whole(ownb)]
                 + [k_spec(g) for g in range(n_per)] + [v_spec(g) for g in range(n_per)],
        out_specs=pl.BlockSpec((1, dec_t, eb), lambda b, s, pt: (b, 0, 0)),
        scratch_shapes=[pltpu.VMEM((ht, eb), bf16), pltpu.VMEM((n_steps, ht, wide), f32),
                        pltpu.VMEM((ht, eb), f32), pltpu.VMEM((ht, V7X_LANES), f32)],
    )
    return pl.pallas_call(
        functools.partial(_attn_sample_kernel, n_per=n_per, n_steps=n_steps, head_dim=head_dim, dec_t=dec_t),
        grid_spec=grid_spec,
        out_shape=jax.ShapeDtypeStruct((nb, dec_t, eb), bf16),
        compiler_params=_params(2),
        name="attn_sample",
    )(page_table, q, kn, vn, sz, far, lastb, ownb, *([cache_k] * n_per), *([cache_v] * n_per))


def _out_proj_kernel(x_ref, t_ref, w_ref, y_ref):
    y_ref[...] = x_ref[...] + _dot(t_ref[...], w_ref[...])


def _out_proj(x, t, w):
    n, d = x.shape
    tm = min(OUT_TILE, n)
    assert n % tm == 0
    return pl.pallas_call(
        _out_proj_kernel,
        grid=(n // tm,),
        in_specs=[pl.BlockSpec((tm, d), lambda i: (i, 0)), pl.BlockSpec((tm, t.shape[1]), lambda i: (i, 0)),
                  _resident(w.shape)],
        out_specs=pl.BlockSpec((tm, d), lambda i: (i, 0)),
        out_shape=jax.ShapeDtypeStruct((n, d), f32),
        compiler_params=_params(1),
        name="out_proj",
    )(x, t, w)


def kernel(x_prompt, x_sample, cache_k, cache_v, page_table, g_norm, rel_bias, w_in_a, g_v_a, w_s_a, b_s_a,
           w_out_a, w_in_b, g_q_b, g_k_b, w_out_b):
    batch, seq, d = x_prompt.shape
    n_dec, dec_t, _ = x_sample.shape
    depth = g_norm.shape[0]
    n_buckets, n_heads = rel_bias.shape
    head_dim = g_q_b.shape[1]
    eb = n_heads * head_dim
    e_a = g_v_a.shape[1]
    n_groups = w_s_a.shape[1]
    n_phys, page_size = cache_k.shape[1], cache_k.shape[2]
    past_len = page_table.shape[1] * page_size
    assert seq % MOBA_BLOCK == 0 and past_len % MOBA_BLOCK == 0 and CHUNK % dec_t == 0
    assert head_dim & (head_dim - 1) == 0 and V7X_LANES % head_dim == 0

    breaks, top = _bucket_breaks(n_buckets, 2 * MOBA_BLOCK + dec_t)
    heads_per = V7X_LANES // head_dim
    bias_tiles = pl.pallas_call(
        functools.partial(_prompt_bias_kernel, breaks=breaks, top=top),
        grid=(n_heads,),
        in_specs=[pl.BlockSpec(memory_space=pltpu.SMEM)],
        out_specs=pl.BlockSpec((1, 2, MOBA_BLOCK, MOBA_BLOCK), lambda h: (h, 0, 0, 0)),
        out_shape=jax.ShapeDtypeStruct((n_heads, 2, MOBA_BLOCK, MOBA_BLOCK), f32),
        compiler_params=_params(1),
        name="prompt_bias",
    )(rel_bias)
    ht = n_heads * dec_t
    rbt = jnp.repeat(rel_bias.T, dec_t, axis=0)
    lastb, ownb, far = pl.pallas_call(
        functools.partial(_sample_bias_kernel, breaks=breaks, top=top, dec_t=dec_t),
        out_shape=[jax.ShapeDtypeStruct((ht, MOBA_BLOCK), f32), jax.ShapeDtypeStruct((ht, page_size), f32),
                   jax.ShapeDtypeStruct((ht, V7X_LANES), f32)],
        name="sample_bias",
    )(rbt)

    head_of_lane = jnp.arange(eb, dtype=jnp.int32) // head_dim
    ind = (head_of_lane[:, None] == jnp.arange(V7X_LANES, dtype=jnp.int32)[None, :]).astype(bf16)
    indt = ind.T

    cache_kt = jnp.transpose(cache_k, (0, 1, 3, 4, 2)).reshape(cache_k.shape[0], n_phys, eb, page_size)
    cache_vt = jnp.transpose(cache_v, (0, 1, 3, 4, 2)).reshape(cache_v.shape[0], n_phys, eb, page_size)

    xp = x_prompt.reshape(batch * seq, d)
    xs = x_sample.reshape(n_dec * dec_t, d)
    k_p, v_p, k_s, v_s, vch = [], [], [], [], []
    for i in range(depth):
        j = i // 2
        gn = g_norm[i][None, :]
        if i % 2 == 0:
            w_in = w_in_a[j].astype(bf16)
            w_out = w_out_a[j].astype(bf16)
            gv = g_v_a[j][None, :]
            gd = e_a // n_groups
            b_prompt = jnp.repeat(b_s_a[j].T, gd, axis=1)
            b_sample = jnp.repeat(jnp.tile(b_s_a[j][:, :dec_t], (1, CHUNK // dec_t)).T, gd, axis=1)
            w_sample = jnp.tile(w_s_a[j][:, :dec_t, :dec_t], (1, CHUNK // dec_t, CHUNK // dec_t))
            xp = _layer_a(xp, gn, w_in, gv, w_s_a[j], b_prompt, w_out, sub_len=CHUNK, emit_v=False)
            xs, v_rows = _layer_a(xs, gn, w_in, gv, w_sample, b_sample, w_out, sub_len=dec_t, emit_v=True)
            vch.append(v_rows.reshape(n_dec, dec_t, e_a))
        else:
            w_in = w_in_b[j].astype(bf16)
            w_t = w_in[:, :3 * eb].T
            w_z = w_in[:, 3 * eb:]
            w_out = w_out_b[j].astype(bf16)
            gq = jnp.tile(g_q_b[j], n_heads)[None, :]
            gk = jnp.tile(g_k_b[j], n_heads)[None, :]
            gq_t = jnp.broadcast_to(gq.T, (eb, V7X_LANES))
            gk_t = jnp.broadcast_to(gk.T, (eb, V7X_LANES))
            ktp, vtp, qtb, vtb, ktok, szp = _proj_b_prompt(xp, gn, w_t, w_z, gq_t, gk_t,
                                                           batch=batch, seq=seq, head_dim=head_dim)
            tp = _attn_prompt(rel_bias, qtb, ktok, vtb, bias_tiles, szp,
                              batch=batch, seq=seq, head_dim=head_dim, top=top)
            qs, kns, vns, szs, kts, vts = _proj_b_sample(xs.reshape(n_dec, dec_t * d), gn, w_in, w_t, gq, gk, gk_t,
                                                        ind, indt, dec_t=dec_t, head_dim=head_dim)
            r3 = lambda a: a.reshape(n_dec, dec_t, eb)
            ts = _attn_sample(page_table, r3(qs), r3(kns), r3(vns), r3(szs), far, lastb, ownb,
                              cache_kt, cache_vt, layer=j, head_dim=head_dim)
            xp = _out_proj(xp, tp, w_out)
            xs = _out_proj(xs, ts.reshape(n_dec * dec_t, eb), w_out)
            k_p.append(ktp)
            v_p.append(vtp)
            k_s.append(kts)
            v_s.append(vts)
    prompt_kv = lambda parts: jnp.transpose(
        jnp.stack(parts).reshape(len(parts), batch, n_heads, head_dim, seq), (0, 1, 4, 2, 3))
    sample_kv = lambda parts: jnp.transpose(
        jnp.stack(parts).reshape(len(parts), dec_t, n_heads, head_dim, n_dec), (0, 4, 1, 2, 3))
    return (xp.reshape(batch, seq, d), xs.reshape(n_dec, dec_t, d), prompt_kv(k_p), prompt_kv(v_p),
            sample_kv(k_s), sample_kv(v_s), jnp.stack(vch))
```

```python
import functools
import math

import numpy as np
import jax
import jax.numpy as jnp
from jax import lax
from jax.experimental import pallas as pl
from jax.experimental.pallas import tpu as pltpu

EPS = 1e-6
CHUNK = 128
MOBA_BLOCK = 256
MOBA_TOPK = 3
MAX_DISTANCE = 128
NEG = -1e30
LOG2E = math.log2(math.e)
V7X_LANES = 128
V7X_VMEM_LIMIT_BYTES = 56 * 1024 * 1024
TOKEN_TILE = 256
OUT_TILE = 512

bf16 = jnp.bfloat16
f32 = jnp.float32


def _dot(a, b):
    return jnp.dot(a, b, preferred_element_type=f32)


def _dot_nt(a, b):
    return lax.dot_general(a, b, (((1,), (1,)), ((), ())), preferred_element_type=f32)


def _gelu(x):
    return 0.5 * x * (1.0 + lax.erf(x * (2.0 ** -0.5)))


def _silu(x):
    return x * jax.nn.sigmoid(x)


def _rms_rows(x, g):
    return x * lax.rsqrt(jnp.mean(x * x, axis=-1, keepdims=True) + EPS) * g


def _params(n_axes):
    return pltpu.CompilerParams(dimension_semantics=("arbitrary",) * n_axes,
                                vmem_limit_bytes=V7X_VMEM_LIMIT_BYTES)


def _resident(shape):
    nd = len(shape)
    return pl.BlockSpec(shape, lambda *_: (0,) * nd, pipeline_mode=pl.Buffered(1))


def _rank_rows(gate, n_valid):
    bid = lax.broadcasted_iota(jnp.int32, (gate.shape[0], 1), 0)
    rank = jnp.zeros(gate.shape, jnp.int32)
    for jp in range(n_valid):
        rowv = gate[jp:jp + 1, :]
        beats = (rowv > gate) | ((rowv == gate) & (jp < bid))
        rank = rank + jnp.where(beats, 1, 0)
    return rank, bid


def _bucket_breaks(n_buckets, max_n):
    n = np.arange(max_n + 1, dtype=np.int32)
    max_exact = n_buckets // 2
    nf = np.maximum(n, 1).astype(np.float32)
    large = max_exact + (np.log(nf / np.float32(max_exact)) / np.float32(math.log(MAX_DISTANCE / max_exact))
                         * np.float32(n_buckets - max_exact)).astype(np.int32)
    b = np.where(n < max_exact, n, np.minimum(large, n_buckets - 1))
    assert np.all(np.diff(b) >= 0)
    breaks = []
    for v in np.unique(b)[:-1]:
        breaks.append((int(v), int(np.max(n[b == v]))))
    top = int(b[-1])
    assert int(b[MOBA_BLOCK + 1]) == top
    return breaks, top


def _bias_of_dist(n, value_of_bucket, breaks, top):
    val = jnp.where(n <= breaks[-1][1], value_of_bucket(breaks[-1][0]), value_of_bucket(top))
    for bv, last in reversed(breaks[:-1]):
        val = jnp.where(n <= last, value_of_bucket(bv), val)
    return val


def _prompt_bias_kernel(rb_ref, o_ref, *, breaks, top):
    h = pl.program_id(0)
    key = lax.broadcasted_iota(jnp.int32, (MOBA_BLOCK, MOBA_BLOCK), 0)
    qry = lax.broadcasted_iota(jnp.int32, (MOBA_BLOCK, MOBA_BLOCK), 1)
    for kind in (0, 1):
        n = qry - key + kind * MOBA_BLOCK
        val = _bias_of_dist(n, lambda b: rb_ref[b, h], breaks, top) * LOG2E
        if kind == 0:
            val = jnp.where(n < 0, NEG, val)
        o_ref[0, kind] = val.astype(f32)


def _sample_bias_kernel(rbt_ref, last_ref, own_ref, far_ref, *, breaks, top, dec_t):
    ht = rbt_ref.shape[0]
    t = lax.broadcasted_iota(jnp.int32, (ht, 1), 0) & (dec_t - 1)
    col = lambda b: rbt_ref[:, b:b + 1]
    c = lax.broadcasted_iota(jnp.int32, (1, last_ref.shape[1]), 1)
    last_ref[...] = _bias_of_dist(MOBA_BLOCK + t - c, col, breaks, top)
    c = lax.broadcasted_iota(jnp.int32, (1, own_ref.shape[1]), 1)
    n = t - c
    own_ref[...] = jnp.where(n < 0, NEG, _bias_of_dist(n, col, breaks, top))
    far_ref[...] = jnp.broadcast_to(col(top), far_ref.shape)


def _layer_a_kernel(x_ref, gn_ref, win_ref, gv_ref, wmix_ref, bmix_ref, wout_ref, y_ref, *rest,
                    sub_len, emit_v):
    if emit_v:
        v_ref, vn_scr = rest
    else:
        (vn_scr,) = rest
    tm = x_ref.shape[0]
    e = gv_ref.shape[1]
    n_groups = wmix_ref.shape[0]
    gd = e // n_groups
    n_chunks = tm // CHUNK
    wide = 2 * V7X_LANES
    g_per = wide // gd

    x = x_ref[...]
    h = _rms_rows(x, gn_ref[...]).astype(bf16)

    vg = _gelu(_dot(h, win_ref[:, e:2 * e]))
    vn = _rms_rows(vg, gv_ref[...])
    if emit_v:
        v_ref[...] = vn
    vn_scr[...] = vn.astype(bf16)

    r = lax.broadcasted_iota(jnp.int32, (CHUNK, CHUNK), 0)
    c = lax.broadcasted_iota(jnp.int32, (CHUNK, CHUNK), 1)
    keep = r >= c
    if sub_len != CHUNK:
        keep = keep & ((r ^ c) < sub_len)
    bias_rows = [bmix_ref[...]] * n_chunks

    acc = jnp.zeros(y_ref.shape, f32)
    for p in range(e // wide):
        lo = p * wide
        u = _gelu(_dot(h, win_ref[:, lo:lo + wide]))
        z = _dot(h, win_ref[:, 2 * e + lo:2 * e + lo + wide])
        parts = []
        for gg in range(g_per):
            g = p * g_per + gg
            wm = jnp.where(keep, wmix_ref[g], 0.0).astype(bf16)
            rhs = jnp.concatenate(
                [vn_scr[k * CHUNK:(k + 1) * CHUNK, g * gd:(g + 1) * gd] for k in range(n_chunks)], axis=1)
            sg = _dot(wm, rhs)
            parts.append(jnp.concatenate([sg[:, k * gd:(k + 1) * gd] for k in range(n_chunks)], axis=0))
        s = jnp.concatenate(parts, axis=1) + jnp.concatenate(
            [b[:, lo:lo + wide] for b in bias_rows], axis=0)
        t = (u * s * _silu(z)).astype(bf16)
        acc = acc + _dot(t, wout_ref[lo:lo + wide, :])
    y_ref[...] = x + acc


def _layer_a(x, gn, w_in, g_v, w_mix, b_mix, w_out, *, sub_len, emit_v):
    n, d = x.shape
    e = g_v.shape[1]
    tm = TOKEN_TILE
    assert n % tm == 0 and tm % CHUNK == 0 and e % (2 * V7X_LANES) == 0
    row_spec = lambda width: pl.BlockSpec((tm, width), lambda i: (i, 0))
    out_shape = [jax.ShapeDtypeStruct((n, d), f32)]
    out_specs = [row_spec(d)]
    if emit_v:
        out_shape.append(jax.ShapeDtypeStruct((n, e), f32))
        out_specs.append(row_spec(e))
    res = pl.pallas_call(
        functools.partial(_layer_a_kernel, sub_len=sub_len, emit_v=emit_v),
        grid=(n // tm,),
        in_specs=[row_spec(d), _resident(gn.shape), _resident(w_in.shape), _resident(g_v.shape),
                  _resident(w_mix.shape), _resident(b_mix.shape), _resident(w_out.shape)],
        out_specs=out_specs,
        out_shape=out_shape,
        scratch_shapes=[pltpu.VMEM((tm, e), bf16)],
        compiler_params=_params(1),
        name="layer_a_sample" if emit_v else "layer_a_prompt",
    )(x, gn, w_in, g_v, w_mix, b_mix, w_out)
    return res if emit_v else res[0]


def _head_norm_t(a, g_ref, head_dim):
    eb, tm = a.shape
    a3 = a.reshape(eb // head_dim, head_dim, tm)
    rs = lax.rsqrt(jnp.mean(a3 * a3, axis=1, keepdims=True) + EPS)
    g = jnp.concatenate([g_ref[...]] * (tm // V7X_LANES), axis=1).reshape(a3.shape)
    return (a3 * rs * g).reshape(eb, tm)


def _head_norm_rows(a, g_ref, ind_ref, indt_ref, head_dim):
    ss = _dot((a * a).astype(bf16), ind_ref[...])
    rs = lax.rsqrt(ss * (1.0 / head_dim) + EPS)
    rs_hi = rs.astype(bf16)
    rs_lo = (rs - rs_hi.astype(f32)).astype(bf16)
    scale = _dot(rs_hi, indt_ref[...]) + _dot(rs_lo, indt_ref[...])
    return a * scale * g_ref[...]


def _proj_b_prompt_kernel(x_ref, gn_ref, wt_ref, wz_ref, gq_ref, gk_ref, *rest, head_dim, n_prev):
    prev = rest[:2 if n_prev else 0]
    kt_ref, vt_ref, qtb_ref, vtb_ref, ktok_ref, sz_ref = rest[len(prev):]
    eb = wz_ref.shape[1]
    h = _rms_rows(x_ref[...], gn_ref[...]).astype(bf16)
    qt = _head_norm_t(_dot_nt(wt_ref[0:eb, :], h), gq_ref, head_dim) * (head_dim ** -0.5 * LOG2E)
    qtb_ref[...] = qt.astype(bf16)
    kt = _head_norm_t(_dot_nt(wt_ref[eb:2 * eb, :], h), gk_ref, head_dim)
    ktok_ref[...] = kt.T.astype(bf16)
    vt = _dot_nt(wt_ref[2 * eb:3 * eb, :], h)
    vtb_ref[...] = vt.astype(bf16)
    sz_ref[...] = _silu(_dot(h, wz_ref[...])).astype(bf16)
    if n_prev:
        kt_ref[0:n_prev] = prev[0][...]
        vt_ref[0:n_prev] = prev[1][...]
    kt_ref[n_prev] = kt
    vt_ref[n_prev] = vt


def _proj_b_prompt(x, gn, w_t, w_z, gq_t, gk_t, prev_kv, *, batch, seq, head_dim):
    n, d = x.shape
    eb = w_z.shape[1]
    tm = TOKEN_TILE
    assert seq % tm == 0 and tm == MOBA_BLOCK
    nt = seq // tm
    n_prev = prev_kv[0].shape[0] if prev_kv else 0
    earlier = lambda: pl.BlockSpec((n_prev, None, eb, tm), lambda i: (0, i // nt, 0, i % nt))
    final_t = lambda: pl.BlockSpec((n_prev + 1, None, eb, tm), lambda i: (0, i // nt, 0, i % nt))
    tile_t = lambda: pl.BlockSpec((None, eb, tm), lambda i: (i, 0, 0))
    rows = lambda: pl.BlockSpec((tm, eb), lambda i: (i, 0))
    stacked = jax.ShapeDtypeStruct((n_prev + 1, batch, eb, seq), f32)
    return pl.pallas_call(
        functools.partial(_proj_b_prompt_kernel, head_dim=head_dim, n_prev=n_prev),
        grid=(n // tm,),
        in_specs=[pl.BlockSpec((tm, d), lambda i: (i, 0)), _resident(gn.shape), _resident(w_t.shape),
                  _resident(w_z.shape), _resident(gq_t.shape), _resident(gk_t.shape)]
                 + [earlier() for _ in prev_kv],
        out_specs=[final_t(), final_t(), tile_t(), tile_t(), rows(), rows()],
        out_shape=[stacked, stacked,
                   jax.ShapeDtypeStruct((n // tm, eb, tm), bf16), jax.ShapeDtypeStruct((n // tm, eb, tm), bf16),
                   jax.ShapeDtypeStruct((n, eb), bf16), jax.ShapeDtypeStruct((n, eb), bf16)],
        compiler_params=_params(1),
        name="proj_b_prompt",
    )(x, gn, w_t, w_z, gq_t, gk_t, *prev_kv)


def _proj_b_sample_kernel(x_ref, gn_ref, w_ref, wt_ref, gq_ref, gk_ref, gkt_ref, ind_ref, indt_ref,
                          q_ref, kn_ref, vn_ref, sz_ref, kt_ref, vt_ref, *, head_dim):
    nseq = x_ref.shape[0]
    eb = gq_ref.shape[1]
    d = gn_ref.shape[1]
    t_per = x_ref.shape[1] // d
    x = jnp.concatenate([x_ref[:, t * d:(t + 1) * d] for t in range(t_per)], axis=0)
    h = _rms_rows(x, gn_ref[...]).astype(bf16)

    def put_rows(ref, a):
        for t in range(t_per):
            ref[:, t * eb:(t + 1) * eb] = a[t * nseq:(t + 1) * nseq].astype(ref.dtype)

    def put_t(ref, a):
        for t in range(t_per):
            ref[t] = a[:, t * nseq:(t + 1) * nseq]

    q = _head_norm_rows(_dot(h, w_ref[:, 0:eb]), gq_ref, ind_ref, indt_ref, head_dim) * (head_dim ** -0.5)
    put_rows(q_ref, q)
    put_rows(kn_ref, _head_norm_rows(_dot(h, w_ref[:, eb:2 * eb]), gk_ref, ind_ref, indt_ref, head_dim))
    put_rows(vn_ref, _dot(h, w_ref[:, 2 * eb:3 * eb]))
    put_rows(sz_ref, _silu(_dot(h, w_ref[:, 3 * eb:4 * eb])))
    put_t(kt_ref, _head_norm_t(_dot_nt(wt_ref[eb:2 * eb, :], h), gkt_ref, head_dim))
    put_t(vt_ref, _dot_nt(wt_ref[2 * eb:3 * eb, :], h))


def _proj_b_sample(x, gn, w, w_t, gq, gk, gk_t, ind, indt, *, dec_t, head_dim):
    nseq = x.shape[0]
    d = gn.shape[1]
    eb = gq.shape[1]
    assert nseq == V7X_LANES and TOKEN_TILE % nseq == 0
    t_per = TOKEN_TILE // nseq
    assert dec_t % t_per == 0
    rows = lambda: pl.BlockSpec((nseq, t_per * eb), lambda i: (0, i))
    sds = lambda dt: jax.ShapeDtypeStruct((nseq, dec_t * eb), dt)
    tsp = lambda: pl.BlockSpec((t_per, eb, nseq), lambda i: (i, 0, 0))
    tsd = jax.ShapeDtypeStruct((dec_t, eb, nseq), f32)
    return pl.pallas_call(
        functools.partial(_proj_b_sample_kernel, head_dim=head_dim),
        grid=(dec_t // t_per,),
        in_specs=[pl.BlockSpec((nseq, t_per * d), lambda i: (0, i)), _resident(gn.shape), _resident(w.shape),
                  _resident(w_t.shape), _resident(gq.shape), _resident(gk.shape), _resident(gk_t.shape),
                  _resident(ind.shape), _resident(indt.shape)],
        out_specs=[rows(), rows(), rows(), rows(), tsp(), tsp()],
        out_shape=[sds(f32), sds(f32), sds(f32), sds(bf16), tsd, tsd],
        compiler_params=_params(1),
        name="proj_b_sample",
    )(x, gn, w, w_t, gq, gk, gk_t, ind, indt)


def _attn_prompt_kernel(rb_ref, qt_ref, k_ref, vt_ref, bias_ref, sz_ref, o_ref, s_scr, v1_scr, *, head_dim, top):
    hp = pl.program_id(1)
    blk = MOBA_BLOCK
    nb, hw, _ = qt_ref.shape
    heads_per = hw // head_dim
    row = lax.broadcasted_iota(jnp.int32, (hw, 1), 0)
    in_head = [(row >= hh * head_dim) & (row < (hh + 1) * head_dim) for hh in range(heads_per)]
    for hh in range(heads_per):
        for j in range(nb):
            v1_scr[hh, j] = jnp.where(in_head[hh], vt_ref[j], jnp.ones((hw, blk), bf16))
    nr = -(-nb // 8) * 8
    ksum = jnp.concatenate([jnp.sum(k_ref[j * blk:(j + 1) * blk, :].astype(f32), axis=0, keepdims=True)
                            for j in range(nb)] + [jnp.zeros((nr - nb, hw), f32)] * (nr > nb), axis=0)
    ksum_hi = ksum.astype(bf16)
    ksum_hl = jnp.concatenate([ksum_hi, (ksum - ksum_hi.astype(f32)).astype(bf16)], axis=0)
    pair = lambda i, j: i * (i + 1) // 2 + j
    shifts = {}
    for i in range(nb):
        qt = qt_ref[i]
        for hh in range(heads_per):
            qm = jnp.where(in_head[hh], qt, jnp.zeros_like(qt))
            c_far = rb_ref[top, hp * heads_per + hh] * LOG2E
            maxes = []
            for j in range(i + 1):
                s = _dot(k_ref[j * blk:(j + 1) * blk, :], qm)
                if j == i:
                    s = s + bias_ref[hh, 0]
                elif j == i - 1:
                    s = s + bias_ref[hh, 1]
                s_scr[hh, pair(i, j)] = s
                mx = jnp.max(s, axis=0, keepdims=True)
                maxes.append(mx + c_far if j < i - 1 else mx)
            chosen = [None] * i
            if i > MOBA_TOPK:
                g2 = _dot(ksum_hl, qm)
                rank, _ = _rank_rows(g2[0:nr] + g2[nr:2 * nr], i)
                chosen = [rank[j:j + 1, :] < MOBA_TOPK for j in range(i)]
            m = maxes[i]
            for j in range(i):
                m = jnp.maximum(m, maxes[j] if chosen[j] is None else jnp.where(chosen[j], maxes[j], NEG))
            for j in range(i + 1):
                shift = m - c_far if j < i - 1 else m
                if j < i and chosen[j] is not None:
                    shift = jnp.where(chosen[j], shift, -NEG)
                shifts[i, hh, j] = shift
    for i in range(nb):
        outs = []
        for hh in range(heads_per):
            acc = jnp.zeros((hw, blk), f32)
            for j in range(i + 1):
                p = jnp.exp2(s_scr[hh, pair(i, j)] - shifts[i, hh, j])
                acc = acc + _dot(v1_scr[hh, j], p.astype(bf16))
            other = ((hh + 1) % heads_per) * head_dim
            outs.append(acc / acc[other:other + 1, :])
        ot = outs[-1]
        for hh in range(heads_per - 2, -1, -1):
            ot = jnp.where(row < (hh + 1) * head_dim, outs[hh], ot)
        rows = slice(i * blk, (i + 1) * blk)
        o_ref[rows, :] = (ot.T * sz_ref[rows, :].astype(f32)).astype(bf16)


def _attn_prompt(rel_bias, qtb, ktok, vtb, bias_tiles, sz, *, batch, seq, head_dim, top):
    n, eb = ktok.shape
    blk = MOBA_BLOCK
    nb = seq // blk
    hw = V7X_LANES
    heads_per = hw // head_dim
    assert heads_per >= 2
    by_block = lambda a: a.reshape(batch, nb, eb, blk)
    seq_rows = lambda: pl.BlockSpec((seq, hw), lambda b, hp: (b, hp))
    seq_t = lambda: pl.BlockSpec((None, nb, hw, blk), lambda b, hp: (b, 0, hp, 0))
    return pl.pallas_call(
        functools.partial(_attn_prompt_kernel, head_dim=head_dim, top=top),
        grid=(batch, eb // hw),
        in_specs=[pl.BlockSpec(memory_space=pltpu.SMEM), seq_t(), seq_rows(), seq_t(),
                  pl.BlockSpec((heads_per, 2, blk, blk), lambda b, hp: (hp, 0, 0, 0)), seq_rows()],
        out_specs=seq_rows(),
        out_shape=jax.ShapeDtypeStruct((n, eb), bf16),
        scratch_shapes=[pltpu.VMEM((heads_per, nb * (nb + 1) // 2, blk, blk), f32),
                        pltpu.VMEM((heads_per, nb, hw, blk), bf16)],
        compiler_params=_params(2),
        name="attn_prompt",
    )(rel_bias, by_block(qtb), ktok, by_block(vtb), bias_tiles, sz)


def _attn_sample_kernel(pt_ref, q_ref, kn_ref, vn_ref, sz_ref, far_ref, lastb_ref, ownb_ref, *rest,
                        n_pages, head_dim, dec_t):
    del pt_ref
    k_refs = rest[:n_pages]
    v_refs = rest[n_pages:2 * n_pages]
    o_ref, p_scr, pown_scr, inv_scr = rest[2 * n_pages:]
    eb, ps = k_refs[0].shape
    ht = p_scr.shape[0]
    n_heads = eb // head_dim
    ppb = MOBA_BLOCK // ps
    nblk = n_pages // ppb
    step = pl.program_id(1)
    lane_head = lax.broadcasted_iota(jnp.int32, (1, eb), 1) >> (head_dim.bit_length() - 1)
    pad_rows = lambda a: jnp.concatenate([a, jnp.zeros((ps - dec_t, eb), f32)], axis=0).astype(bf16)
    block = lambda refs, j: jnp.concatenate([refs[j * ppb + g][...].astype(bf16) for g in range(ppb)], axis=1)
    cols = lambda j: slice(j * MOBA_BLOCK, (j + 1) * MOBA_BLOCK)
    lane_tiles = lambda a: [a[:, q * V7X_LANES:(q + 1) * V7X_LANES] for q in range(a.shape[1] // V7X_LANES)]

    @pl.when(step == 0)
    def _():
        qt = jnp.concatenate([q_ref[0]] * n_heads, axis=0)
        row_head = lax.broadcasted_iota(jnp.int32, (ht, 1), 0) >> (dec_t.bit_length() - 1)
        qbd = jnp.where(row_head == lane_head, qt, 0.0).astype(bf16)

        lane = lax.broadcasted_iota(jnp.int32, (1, V7X_LANES), 1)
        gm = jnp.zeros((ht, V7X_LANES), f32)
        for j in range(nblk):
            sj = _dot(qbd, block(k_refs, j))
            p_scr[:, cols(j)] = sj
            gm = jnp.where(lane == j, jnp.sum(sj, axis=1, keepdims=True), gm)
        nrow = -(-nblk // 8) * 8
        rank, bid = _rank_rows(gm.T[0:nrow, :], nblk)
        sel = jnp.where((bid < nblk) & (rank < MOBA_TOPK), 1.0, 0.0)
        selm = jnp.concatenate([sel, jnp.zeros((V7X_LANES - nrow, ht), f32)], axis=0).T

        s_own = _dot_nt(qbd, pad_rows(kn_ref[0])) + ownb_ref[...]
        far = jnp.concatenate([far_ref[...]] * (MOBA_BLOCK // V7X_LANES), axis=1)
        mx = s_own
        for j in range(nblk):
            sj = p_scr[:, cols(j)] + (lastb_ref[...] if j == nblk - 1 else far)
            sj = jnp.where(selm[:, j:j + 1] > 0.5, sj, NEG)
            p_scr[:, cols(j)] = sj
            for tile in lane_tiles(sj):
                mx = jnp.maximum(mx, tile)
        mx = jnp.max(mx, axis=1, keepdims=True)
        p_own = jnp.exp(s_own - mx)
        lsum = p_own
        for j in range(nblk):
            pj = jnp.exp(p_scr[:, cols(j)] - mx)
            p_scr[:, cols(j)] = pj
            for tile in lane_tiles(pj):
                lsum = lsum + tile
        pown_scr[...] = p_own
        inv_scr[...] = jnp.broadcast_to(1.0 / jnp.sum(lsum, axis=1, keepdims=True), inv_scr.shape)

    @pl.when(step == 1)
    def _():
        acc = _dot(pown_scr[...].astype(bf16), pad_rows(vn_ref[0]))
        for j in range(nblk):
            acc = acc + _dot_nt(p_scr[:, cols(j)].astype(bf16), block(v_refs, j))
        inv = jnp.concatenate([inv_scr[...]] * (eb // V7X_LANES), axis=1)
        o = acc * inv
        attn = jnp.zeros((dec_t, eb), f32)
        for hh in range(n_heads):
            attn = attn + jnp.where(lane_head == hh, o[hh * dec_t:(hh + 1) * dec_t, :], 0.0)
        o_ref[0] = (attn * sz_ref[0].astype(f32)).astype(bf16)


def _attn_sample(page_table, q, kn, vn, sz, far, lastb, ownb, cache_k, cache_v, *, layer, head_dim):
    nb, dec_t, eb = q.shape
    n_pages = page_table.shape[1]
    ps = cache_k.shape[3]
    assert MOBA_BLOCK % ps == 0 and (n_pages * ps) % MOBA_BLOCK == 0 and ps == V7X_LANES
    ht = (eb // head_dim) * dec_t
    assert ht == V7X_LANES and dec_t & (dec_t - 1) == 0 and dec_t <= ps and n_pages * ps // MOBA_BLOCK <= ht
    per_b = lambda: pl.BlockSpec((1, dec_t, eb), lambda b, s, pt: (b, 0, 0))
    whole = lambda a: pl.BlockSpec(a.shape, lambda b, s, pt: (0,) * a.ndim)

    def k_spec(g):
        return pl.BlockSpec((None, None, eb, ps), lambda b, s, pt: (layer, pt[b, g], 0, 0))

    def v_spec(g):
        return pl.BlockSpec((None, None, eb, ps), lambda b, s, pt: (layer, pt[jnp.maximum(b + s - 1, 0), g], 0, 0))

    grid_spec = pltpu.PrefetchScalarGridSpec(
        num_scalar_prefetch=1,
        grid=(nb, 2),
        in_specs=[per_b(), per_b(), per_b(), per_b(), whole(far), whole(lastb), whole(ownb)]
                 + [k_spec(g) for g in range(n_pages)] + [v_spec(g) for g in range(n_pages)],
        out_specs=pl.BlockSpec((1, dec_t, eb), lambda b, s, pt: (b, 0, 0)),
        scratch_shapes=[pltpu.VMEM((ht, n_pages * ps), f32), pltpu.VMEM((ht, ps), f32),
                        pltpu.VMEM((ht, V7X_LANES), f32)],
    )
    return pl.pallas_call(
        functools.partial(_attn_sample_kernel, n_pages=n_pages, head_dim=head_dim, dec_t=dec_t),
        grid_spec=grid_spec,
        out_shape=jax.ShapeDtypeStruct((nb, dec_t, eb), bf16),
        compiler_params=_params(2),
        name="attn_sample",
    )(page_table, q, kn, vn, sz, far, lastb, ownb, *([cache_k] * n_pages), *([cache_v] * n_pages))


def _out_proj_kernel(x_ref, t_ref, w_ref, y_ref):
    y_ref[...] = x_ref[...] + _dot(t_ref[...], w_ref[...])


def _out_proj(x, t, w):
    n, d = x.shape
    tm = min(OUT_TILE, n)
    assert n % tm == 0
    return pl.pallas_call(
        _out_proj_kernel,
        grid=(n // tm,),
        in_specs=[pl.BlockSpec((tm, d), lambda i: (i, 0)), pl.BlockSpec((tm, t.shape[1]), lambda i: (i, 0)),
                  _resident(w.shape)],
        out_specs=pl.BlockSpec((tm, d), lambda i: (i, 0)),
        out_shape=jax.ShapeDtypeStruct((n, d), f32),
        compiler_params=_params(1),
        name="out_proj",
    )(x, t, w)


def kernel(x_prompt, x_sample, cache_k, cache_v, page_table, g_norm, rel_bias, w_in_a, g_v_a, w_s_a, b_s_a,
           w_out_a, w_in_b, g_q_b, g_k_b, w_out_b):
    batch, seq, d = x_prompt.shape
    n_dec, dec_t, _ = x_sample.shape
    depth = g_norm.shape[0]
    n_buckets, n_heads = rel_bias.shape
    head_dim = g_q_b.shape[1]
    eb = n_heads * head_dim
    e_a = g_v_a.shape[1]
    n_groups = w_s_a.shape[1]
    n_phys, page_size = cache_k.shape[1], cache_k.shape[2]
    past_len = page_table.shape[1] * page_size
    assert seq % MOBA_BLOCK == 0 and past_len % MOBA_BLOCK == 0 and CHUNK % dec_t == 0
    assert head_dim & (head_dim - 1) == 0 and V7X_LANES % head_dim == 0

    breaks, top = _bucket_breaks(n_buckets, 2 * MOBA_BLOCK + dec_t)
    heads_per = V7X_LANES // head_dim
    bias_tiles = pl.pallas_call(
        functools.partial(_prompt_bias_kernel, breaks=breaks, top=top),
        grid=(n_heads,),
        in_specs=[pl.BlockSpec(memory_space=pltpu.SMEM)],
        out_specs=pl.BlockSpec((1, 2, MOBA_BLOCK, MOBA_BLOCK), lambda h: (h, 0, 0, 0)),
        out_shape=jax.ShapeDtypeStruct((n_heads, 2, MOBA_BLOCK, MOBA_BLOCK), f32),
        compiler_params=_params(1),
        name="prompt_bias",
    )(rel_bias)
    ht = n_heads * dec_t
    rbt = jnp.repeat(rel_bias.T, dec_t, axis=0)
    lastb, ownb, far = pl.pallas_call(
        functools.partial(_sample_bias_kernel, breaks=breaks, top=top, dec_t=dec_t),
        out_shape=[jax.ShapeDtypeStruct((ht, MOBA_BLOCK), f32), jax.ShapeDtypeStruct((ht, page_size), f32),
                   jax.ShapeDtypeStruct((ht, V7X_LANES), f32)],
        name="sample_bias",
    )(rbt)

    head_of_lane = jnp.arange(eb, dtype=jnp.int32) // head_dim
    ind = (head_of_lane[:, None] == jnp.arange(V7X_LANES, dtype=jnp.int32)[None, :]).astype(bf16)
    indt = ind.T

    cache_kt = jnp.transpose(cache_k, (0, 1, 3, 4, 2)).reshape(cache_k.shape[0], n_phys, eb, page_size)
    cache_vt = jnp.transpose(cache_v, (0, 1, 3, 4, 2)).reshape(cache_v.shape[0], n_phys, eb, page_size)

    xp = x_prompt.reshape(batch * seq, d)
    xs = x_sample.reshape(n_dec * dec_t, d)
    prompt_kv, k_s, v_s, vch = [], [], [], []
    for i in range(depth):
        j = i // 2
        gn = g_norm[i][None, :]
        if i % 2 == 0:
            w_in = w_in_a[j].astype(bf16)
            w_out = w_out_a[j].astype(bf16)
            gv = g_v_a[j][None, :]
            gd = e_a // n_groups
            b_prompt = jnp.repeat(b_s_a[j].T, gd, axis=1)
            b_sample = jnp.repeat(jnp.tile(b_s_a[j][:, :dec_t], (1, CHUNK // dec_t)).T, gd, axis=1)
            w_sample = jnp.tile(w_s_a[j][:, :dec_t, :dec_t], (1, CHUNK // dec_t, CHUNK // dec_t))
            xp = _layer_a(xp, gn, w_in, gv, w_s_a[j], b_prompt, w_out, sub_len=CHUNK, emit_v=False)
            xs, v_rows = _layer_a(xs, gn, w_in, gv, w_sample, b_sample, w_out, sub_len=dec_t, emit_v=True)
            vch.append(v_rows.reshape(n_dec, dec_t, e_a))
        else:
            w_in = w_in_b[j].astype(bf16)
            w_t = w_in[:, :3 * eb].T
            w_z = w_in[:, 3 * eb:]
            w_out = w_out_b[j].astype(bf16)
            gq = jnp.tile(g_q_b[j], n_heads)[None, :]
            gk = jnp.tile(g_k_b[j], n_heads)[None, :]
            gq_t = jnp.broadcast_to(gq.T, (eb, V7X_LANES))
            gk_t = jnp.broadcast_to(gk.T, (eb, V7X_LANES))
            *prompt_kv, qtb, vtb, ktok, szp = _proj_b_prompt(xp, gn, w_t, w_z, gq_t, gk_t, prompt_kv,
                                                             batch=batch, seq=seq, head_dim=head_dim)
            tp = _attn_prompt(rel_bias, qtb, ktok, vtb, bias_tiles, szp,
                              batch=batch, seq=seq, head_dim=head_dim, top=top)
            qs, kns, vns, szs, kts, vts = _proj_b_sample(xs.reshape(n_dec, dec_t * d), gn, w_in, w_t, gq, gk, gk_t,
                                                        ind, indt, dec_t=dec_t, head_dim=head_dim)
            r3 = lambda a: a.reshape(n_dec, dec_t, eb)
            ts = _attn_sample(page_table, r3(qs), r3(kns), r3(vns), r3(szs), far, lastb, ownb,
                              cache_kt, cache_vt, layer=j, head_dim=head_dim)
            xp = _out_proj(xp, tp, w_out)
            xs = _out_proj(xs, ts.reshape(n_dec * dec_t, eb), w_out)
            k_s.append(kts)
            v_s.append(vts)
    prompt_out = lambda a: jnp.transpose(a.reshape(a.shape[0], batch, n_heads, head_dim, seq), (0, 1, 4, 2, 3))
    sample_out = lambda parts: jnp.transpose(
        jnp.stack(parts).reshape(len(parts), dec_t, n_heads, head_dim, n_dec), (0, 4, 1, 2, 3))
    return (xp.reshape(batch, seq, d), xs.reshape(n_dec, dec_t, d), prompt_out(prompt_kv[0]),
            prompt_out(prompt_kv[1]), sample_out(k_s), sample_out(v_s), jnp.stack(vch))
```

```python
import functools
import math

import numpy as np
import jax
import jax.numpy as jnp
from jax import lax
from jax.experimental import pallas as pl
from jax.experimental.pallas import tpu as pltpu

EPS = 1e-6
CHUNK = 128
MOBA_BLOCK = 256
MOBA_TOPK = 3
MAX_DISTANCE = 128
NEG = -1e30
LOG2E = math.log2(math.e)
V7X_LANES = 128
V7X_VMEM_LIMIT_BYTES = 56 * 1024 * 1024
TOKEN_TILE = 256
OUT_TILE = 512

bf16 = jnp.bfloat16
f32 = jnp.float32


def _dot(a, b):
    return jnp.dot(a, b, preferred_element_type=f32)


def _dot_nt(a, b):
    return lax.dot_general(a, b, (((1,), (1,)), ((), ())), preferred_element_type=f32)


def _gelu(x):
    return 0.5 * x * (1.0 + lax.erf(x * (2.0 ** -0.5)))


def _silu(x):
    return x * jax.nn.sigmoid(x)


def _rms_rows(x, g):
    return x * lax.rsqrt(jnp.mean(x * x, axis=-1, keepdims=True) + EPS) * g


def _params(n_axes):
    return pltpu.CompilerParams(dimension_semantics=("arbitrary",) * n_axes,
                                vmem_limit_bytes=V7X_VMEM_LIMIT_BYTES)


def _resident(shape):
    nd = len(shape)
    return pl.BlockSpec(shape, lambda *_: (0,) * nd, pipeline_mode=pl.Buffered(1))


def _rank_rows(gate, n_valid):
    bid = lax.broadcasted_iota(jnp.int32, (gate.shape[0], 1), 0)
    rank = jnp.zeros(gate.shape, jnp.int32)
    for jp in range(n_valid):
        rowv = gate[jp:jp + 1, :]
        beats = (rowv > gate) | ((rowv == gate) & (jp < bid))
        rank = rank + jnp.where(beats, 1, 0)
    return rank, bid


def _bucket_breaks(n_buckets, max_n):
    n = np.arange(max_n + 1, dtype=np.int32)
    max_exact = n_buckets // 2
    nf = np.maximum(n, 1).astype(np.float32)
    large = max_exact + (np.log(nf / np.float32(max_exact)) / np.float32(math.log(MAX_DISTANCE / max_exact))
                         * np.float32(n_buckets - max_exact)).astype(np.int32)
    b = np.where(n < max_exact, n, np.minimum(large, n_buckets - 1))
    assert np.all(np.diff(b) >= 0)
    breaks = []
    for v in np.unique(b)[:-1]:
        breaks.append((int(v), int(np.max(n[b == v]))))
    top = int(b[-1])
    assert int(b[MOBA_BLOCK + 1]) == top
    return breaks, top


def _bias_of_dist(n, value_of_bucket, breaks, top):
    val = jnp.where(n <= breaks[-1][1], value_of_bucket(breaks[-1][0]), value_of_bucket(top))
    for bv, last in reversed(breaks[:-1]):
        val = jnp.where(n <= last, value_of_bucket(bv), val)
    return val


def _prompt_bias_kernel(rb_ref, o_ref, *, breaks, top):
    h = pl.program_id(0)
    key = lax.broadcasted_iota(jnp.int32, (MOBA_BLOCK, MOBA_BLOCK), 0)
    qry = lax.broadcasted_iota(jnp.int32, (MOBA_BLOCK, MOBA_BLOCK), 1)
    for kind in (0, 1):
        n = qry - key + kind * MOBA_BLOCK
        val = _bias_of_dist(n, lambda b: rb_ref[b, h], breaks, top) * LOG2E
        if kind == 0:
            val = jnp.where(n < 0, NEG, val)
        o_ref[0, kind] = val.astype(f32)


def _sample_bias_kernel(rbt_ref, last_ref, own_ref, far_ref, *, breaks, top, dec_t):
    ht = rbt_ref.shape[0]
    t = lax.broadcasted_iota(jnp.int32, (ht, 1), 0) & (dec_t - 1)
    col = lambda b: rbt_ref[:, b:b + 1]
    c = lax.broadcasted_iota(jnp.int32, (1, last_ref.shape[1]), 1)
    last_ref[...] = _bias_of_dist(MOBA_BLOCK + t - c, col, breaks, top)
    c = lax.broadcasted_iota(jnp.int32, (1, own_ref.shape[1]), 1)
    n = t - c
    own_ref[...] = jnp.where(n < 0, NEG, _bias_of_dist(n, col, breaks, top))
    far_ref[...] = jnp.broadcast_to(col(top), far_ref.shape)


def _layer_a_kernel(x_ref, gn_ref, win_ref, gv_ref, wmix_ref, bmix_ref, wout_ref, y_ref, *rest,
                    sub_len, emit_v):
    if emit_v:
        v_ref, vn_scr = rest
    else:
        (vn_scr,) = rest
    tm = x_ref.shape[0]
    e = gv_ref.shape[1]
    n_groups = wmix_ref.shape[0]
    gd = e // n_groups
    n_chunks = tm // CHUNK
    wide = 2 * V7X_LANES
    g_per = wide // gd

    x = x_ref[...]
    h = _rms_rows(x, gn_ref[...]).astype(bf16)

    vg = _gelu(_dot(h, win_ref[:, e:2 * e]))
    vn = _rms_rows(vg, gv_ref[...])
    if emit_v:
        v_ref[...] = vn
    vn_scr[...] = vn.astype(bf16)

    r = lax.broadcasted_iota(jnp.int32, (CHUNK, CHUNK), 0)
    c = lax.broadcasted_iota(jnp.int32, (CHUNK, CHUNK), 1)
    keep = r >= c
    if sub_len != CHUNK:
        keep = keep & ((r ^ c) < sub_len)
    bias_rows = [bmix_ref[...]] * n_chunks

    acc = jnp.zeros(y_ref.shape, f32)
    for p in range(e // wide):
        lo = p * wide
        u = _gelu(_dot(h, win_ref[:, lo:lo + wide]))
        z = _dot(h, win_ref[:, 2 * e + lo:2 * e + lo + wide])
        parts = []
        for gg in range(g_per):
            g = p * g_per + gg
            wm = jnp.where(keep, wmix_ref[g], 0.0).astype(bf16)
            rhs = jnp.concatenate(
                [vn_scr[k * CHUNK:(k + 1) * CHUNK, g * gd:(g + 1) * gd] for k in range(n_chunks)], axis=1)
            sg = _dot(wm, rhs)
            parts.append(jnp.concatenate([sg[:, k * gd:(k + 1) * gd] for k in range(n_chunks)], axis=0))
        s = jnp.concatenate(parts, axis=1) + jnp.concatenate(
            [b[:, lo:lo + wide] for b in bias_rows], axis=0)
        t = (u * s * _silu(z)).astype(bf16)
        acc = acc + _dot(t, wout_ref[lo:lo + wide, :])
    y_ref[...] = x + acc


def _layer_a(x, gn, w_in, g_v, w_mix, b_mix, w_out, *, sub_len, emit_v):
    n, d = x.shape
    e = g_v.shape[1]
    tm = TOKEN_TILE
    assert n % tm == 0 and tm % CHUNK == 0 and e % (2 * V7X_LANES) == 0
    row_spec = lambda width: pl.BlockSpec((tm, width), lambda i: (i, 0))
    out_shape = [jax.ShapeDtypeStruct((n, d), f32)]
    out_specs = [row_spec(d)]
    if emit_v:
        out_shape.append(jax.ShapeDtypeStruct((n, e), f32))
        out_specs.append(row_spec(e))
    res = pl.pallas_call(
        functools.partial(_layer_a_kernel, sub_len=sub_len, emit_v=emit_v),
        grid=(n // tm,),
        in_specs=[row_spec(d), _resident(gn.shape), _resident(w_in.shape), _resident(g_v.shape),
                  _resident(w_mix.shape), _resident(b_mix.shape), _resident(w_out.shape)],
        out_specs=out_specs,
        out_shape=out_shape,
        scratch_shapes=[pltpu.VMEM((tm, e), bf16)],
        compiler_params=_params(1),
        name="layer_a_sample" if emit_v else "layer_a_prompt",
    )(x, gn, w_in, g_v, w_mix, b_mix, w_out)
    return res if emit_v else res[0]


def _head_norm_t(a, g_ref, head_dim):
    eb, tm = a.shape
    a3 = a.reshape(eb // head_dim, head_dim, tm)
    rs = lax.rsqrt(jnp.mean(a3 * a3, axis=1, keepdims=True) + EPS)
    g = jnp.concatenate([g_ref[...]] * (tm // V7X_LANES), axis=1).reshape(a3.shape)
    return (a3 * rs * g).reshape(eb, tm)


def _head_norm_rows(a, g_ref, ind_ref, indt_ref, head_dim):
    ss = _dot((a * a).astype(bf16), ind_ref[...])
    rs = lax.rsqrt(ss * (1.0 / head_dim) + EPS)
    rs_hi = rs.astype(bf16)
    rs_lo = (rs - rs_hi.astype(f32)).astype(bf16)
    scale = _dot(rs_hi, indt_ref[...]) + _dot(rs_lo, indt_ref[...])
    return a * scale * g_ref[...]


def _proj_b_prompt_kernel(x_ref, gn_ref, wt_ref, wz_ref, gq_ref, gk_ref, *rest, head_dim, n_prev):
    prev = rest[:2 if n_prev else 0]
    kt_ref, vt_ref, qtb_ref, vtb_ref, ktok_ref, sz_ref = rest[len(prev):]
    eb = wz_ref.shape[1]
    h = _rms_rows(x_ref[...], gn_ref[...]).astype(bf16)
    qt = _head_norm_t(_dot_nt(wt_ref[0:eb, :], h), gq_ref, head_dim) * (head_dim ** -0.5 * LOG2E)
    qtb_ref[...] = qt.astype(bf16)
    kt = _head_norm_t(_dot_nt(wt_ref[eb:2 * eb, :], h), gk_ref, head_dim)
    ktok_ref[...] = kt.T.astype(bf16)
    vt = _dot_nt(wt_ref[2 * eb:3 * eb, :], h)
    vtb_ref[...] = vt.astype(bf16)
    sz_ref[...] = _silu(_dot(h, wz_ref[...])).astype(bf16)
    if n_prev:
        kt_ref[0:n_prev] = prev[0][...]
        vt_ref[0:n_prev] = prev[1][...]
    kt_ref[n_prev] = kt
    vt_ref[n_prev] = vt


def _proj_b_prompt(x, gn, w_t, w_z, gq_t, gk_t, prev_kv, *, batch, seq, head_dim):
    n, d = x.shape
    eb = w_z.shape[1]
    tm = TOKEN_TILE
    assert seq % tm == 0 and tm == MOBA_BLOCK
    nt = seq // tm
    n_prev = prev_kv[0].shape[0] if prev_kv else 0
    earlier = lambda: pl.BlockSpec((n_prev, None, eb, tm), lambda i: (0, i // nt, 0, i % nt))
    final_t = lambda: pl.BlockSpec((n_prev + 1, None, eb, tm), lambda i: (0, i // nt, 0, i % nt))
    tile_t = lambda: pl.BlockSpec((None, eb, tm), lambda i: (i, 0, 0))
    rows = lambda: pl.BlockSpec((tm, eb), lambda i: (i, 0))
    stacked = jax.ShapeDtypeStruct((n_prev + 1, batch, eb, seq), f32)
    return pl.pallas_call(
        functools.partial(_proj_b_prompt_kernel, head_dim=head_dim, n_prev=n_prev),
        grid=(n // tm,),
        in_specs=[pl.BlockSpec((tm, d), lambda i: (i, 0)), _resident(gn.shape), _resident(w_t.shape),
                  _resident(w_z.shape), _resident(gq_t.shape), _resident(gk_t.shape)]
                 + [earlier() for _ in prev_kv],
        out_specs=[final_t(), final_t(), tile_t(), tile_t(), rows(), rows()],
        out_shape=[stacked, stacked,
                   jax.ShapeDtypeStruct((n // tm, eb, tm), bf16), jax.ShapeDtypeStruct((n // tm, eb, tm), bf16),
                   jax.ShapeDtypeStruct((n, eb), bf16), jax.ShapeDtypeStruct((n, eb), bf16)],
        compiler_params=_params(1),
        name="proj_b_prompt",
    )(x, gn, w_t, w_z, gq_t, gk_t, *prev_kv)


def _proj_b_sample_kernel(x_ref, gn_ref, w_ref, wt_ref, gq_ref, gk_ref, gkt_ref, ind_ref, indt_ref,
                          q_ref, kn_ref, vn_ref, sz_ref, kt_ref, vt_ref, *, head_dim):
    nseq = x_ref.shape[0]
    eb = gq_ref.shape[1]
    d = gn_ref.shape[1]
    t_per = x_ref.shape[1] // d
    x = jnp.concatenate([x_ref[:, t * d:(t + 1) * d] for t in range(t_per)], axis=0)
    h = _rms_rows(x, gn_ref[...]).astype(bf16)

    def put_rows(ref, a):
        for t in range(t_per):
            ref[:, t * eb:(t + 1) * eb] = a[t * nseq:(t + 1) * nseq].astype(ref.dtype)

    def put_t(ref, a):
        for t in range(t_per):
            ref[t] = a[:, t * nseq:(t + 1) * nseq]

    q = _head_norm_rows(_dot(h, w_ref[:, 0:eb]), gq_ref, ind_ref, indt_ref, head_dim) * (head_dim ** -0.5)
    put_rows(q_ref, q)
    put_rows(kn_ref, _head_norm_rows(_dot(h, w_ref[:, eb:2 * eb]), gk_ref, ind_ref, indt_ref, head_dim))
    put_rows(vn_ref, _dot(h, w_ref[:, 2 * eb:3 * eb]))
    put_rows(sz_ref, _silu(_dot(h, w_ref[:, 3 * eb:4 * eb])))
    put_t(kt_ref, _head_norm_t(_dot_nt(wt_ref[eb:2 * eb, :], h), gkt_ref, head_dim))
    put_t(vt_ref, _dot_nt(wt_ref[2 * eb:3 * eb, :], h))


def _proj_b_sample(x, gn, w, w_t, gq, gk, gk_t, ind, indt, *, dec_t, head_dim):
    nseq = x.shape[0]
    d = gn.shape[1]
    eb = gq.shape[1]
    assert nseq == V7X_LANES and TOKEN_TILE % nseq == 0
    t_per = TOKEN_TILE // nseq
    assert dec_t % t_per == 0
    rows = lambda: pl.BlockSpec((nseq, t_per * eb), lambda i: (0, i))
    sds = lambda dt: jax.ShapeDtypeStruct((nseq, dec_t * eb), dt)
    tsp = lambda: pl.BlockSpec((t_per, eb, nseq), lambda i: (i, 0, 0))
    tsd = jax.ShapeDtypeStruct((dec_t, eb, nseq), f32)
    return pl.pallas_call(
        functools.partial(_proj_b_sample_kernel, head_dim=head_dim),
        grid=(dec_t // t_per,),
        in_specs=[pl.BlockSpec((nseq, t_per * d), lambda i: (0, i)), _resident(gn.shape), _resident(w.shape),
                  _resident(w_t.shape), _resident(gq.shape), _resident(gk.shape), _resident(gk_t.shape),
                  _resident(ind.shape), _resident(indt.shape)],
        out_specs=[rows(), rows(), rows(), rows(), tsp(), tsp()],
        out_shape=[sds(f32), sds(f32), sds(f32), sds(bf16), tsd, tsd],
        compiler_params=_params(1),
        name="proj_b_sample",
    )(x, gn, w, w_t, gq, gk, gk_t, ind, indt)


def _prompt_head_attention(c_far, hh, qt_ref, k_ref, vt_ref, bias_ref, sz_ref, o_ref, s_scr, v1_scr, o_scr,
                           *, head_dim):
    blk = MOBA_BLOCK
    nb, hw, _ = qt_ref.shape
    heads_per = hw // head_dim
    row = lax.broadcasted_iota(jnp.int32, (hw, 1), 0)
    in_head = (row >= hh * head_dim) & (row < (hh + 1) * head_dim)
    lane = lax.broadcasted_iota(jnp.int32, (1, hw), 1)
    lane_in_head = (lane >= hh * head_dim) & (lane < (hh + 1) * head_dim)
    for j in range(nb):
        v1_scr[j] = jnp.where(in_head, vt_ref[j], jnp.ones((hw, blk), bf16))
    nr = -(-nb // 8) * 8
    ksum = jnp.concatenate([jnp.sum(k_ref[j * blk:(j + 1) * blk, :].astype(f32), axis=0, keepdims=True)
                            for j in range(nb)] + [jnp.zeros((nr - nb, hw), f32)] * (nr > nb), axis=0)
    ksum_hi = ksum.astype(bf16)
    ksum_hl = jnp.concatenate([ksum_hi, (ksum - ksum_hi.astype(f32)).astype(bf16)], axis=0)
    pair = lambda i, j: i * (i + 1) // 2 + j
    shifts = {}
    for i in range(nb):
        qm = jnp.where(in_head, qt_ref[i], jnp.zeros((hw, blk), bf16))
        maxes = []
        for j in range(i + 1):
            s = _dot(k_ref[j * blk:(j + 1) * blk, :], qm)
            if j == i:
                s = s + bias_ref[0]
            elif j == i - 1:
                s = s + bias_ref[1]
            s_scr[pair(i, j)] = s
            mx = jnp.max(s, axis=0, keepdims=True)
            maxes.append(mx + c_far if j < i - 1 else mx)
        chosen = [None] * i
        if i > MOBA_TOPK:
            g2 = _dot(ksum_hl, qm)
            rank, _ = _rank_rows(g2[0:nr] + g2[nr:2 * nr], i)
            chosen = [rank[j:j + 1, :] < MOBA_TOPK for j in range(i)]
        m = maxes[i]
        for j in range(i):
            m = jnp.maximum(m, maxes[j] if chosen[j] is None else jnp.where(chosen[j], maxes[j], NEG))
        for j in range(i + 1):
            shift = m - c_far if j < i - 1 else m
            if j < i and chosen[j] is not None:
                shift = jnp.where(chosen[j], shift, -NEG)
            shifts[i, j] = shift
    for i in range(nb):
        acc = jnp.zeros((hw, blk), f32)
        for j in range(i + 1):
            acc = acc + _dot(v1_scr[j], jnp.exp2(s_scr[pair(i, j)] - shifts[i, j]).astype(bf16))
        den = jnp.max(jnp.where(in_head, 0.0, acc), axis=0, keepdims=True)
        rows = slice(i * blk, (i + 1) * blk)
        t = ((acc / den).T * sz_ref[rows, :].astype(f32)).astype(bf16)
        o_scr[hh, rows, :] = jnp.where(lane_in_head, t, jnp.zeros_like(t))
    total = o_scr[0]
    for other in range(1, heads_per):
        total = total + o_scr[other]
    o_ref[...] = total


def _sample_attention(q_ref, kn_ref, vn_ref, sz_ref, far_ref, lastb_ref, ownb_ref, k_refs, v_refs, o_ref, p_scr,
                      *, head_dim, dec_t):
    eb, ps = k_refs[0].shape
    ht = p_scr.shape[0]
    n_heads = eb // head_dim
    ppb = MOBA_BLOCK // ps
    nblk = len(k_refs) // ppb
    lane_head = lax.broadcasted_iota(jnp.int32, (1, eb), 1) >> (head_dim.bit_length() - 1)
    pad_rows = lambda a: jnp.concatenate([a, jnp.zeros((ps - dec_t, eb), f32)], axis=0).astype(bf16)
    block = lambda refs, j: jnp.concatenate([refs[j * ppb + g][...].astype(bf16) for g in range(ppb)], axis=1)
    cols = lambda j: slice(j * MOBA_BLOCK, (j + 1) * MOBA_BLOCK)
    lane_tiles = lambda a: [a[:, q * V7X_LANES:(q + 1) * V7X_LANES] for q in range(a.shape[1] // V7X_LANES)]

    qt = jnp.concatenate([q_ref[0]] * n_heads, axis=0)
    row_head = lax.broadcasted_iota(jnp.int32, (ht, 1), 0) >> (dec_t.bit_length() - 1)
    qbd = jnp.where(row_head == lane_head, qt, 0.0).astype(bf16)

    lane = lax.broadcasted_iota(jnp.int32, (1, V7X_LANES), 1)
    gm = jnp.zeros((ht, V7X_LANES), f32)
    for j in range(nblk):
        sj = _dot(qbd, block(k_refs, j))
        p_scr[:, cols(j)] = sj
        gm = jnp.where(lane == j, jnp.sum(sj, axis=1, keepdims=True), gm)
    nrow = -(-nblk // 8) * 8
    rank, bid = _rank_rows(gm.T[0:nrow, :], nblk)
    sel = jnp.where((bid < nblk) & (rank < MOBA_TOPK), 1.0, 0.0)
    selm = jnp.concatenate([sel, jnp.zeros((V7X_LANES - nrow, ht), f32)], axis=0).T

    s_own = _dot_nt(qbd, pad_rows(kn_ref[0])) + ownb_ref[...]
    far = jnp.concatenate([far_ref[...]] * (MOBA_BLOCK // V7X_LANES), axis=1)
    mx = s_own
    for j in range(nblk):
        sj = p_scr[:, cols(j)] + (lastb_ref[...] if j == nblk - 1 else far)
        sj = jnp.where(selm[:, j:j + 1] > 0.5, sj, NEG)
        p_scr[:, cols(j)] = sj
        for tile in lane_tiles(sj):
            mx = jnp.maximum(mx, tile)
    mx = jnp.max(mx, axis=1, keepdims=True)
    p_own = jnp.exp(s_own - mx)
    lsum = p_own
    acc = _dot(p_own.astype(bf16), pad_rows(vn_ref[0]))
    for j in range(nblk):
        pj = jnp.exp(p_scr[:, cols(j)] - mx)
        for tile in lane_tiles(pj):
            lsum = lsum + tile
        acc = acc + _dot_nt(pj.astype(bf16), block(v_refs, j))
    o = acc * (1.0 / jnp.sum(lsum, axis=1, keepdims=True))
    attn = jnp.zeros((dec_t, eb), f32)
    for hh in range(n_heads):
        attn = attn + jnp.where(lane_head == hh, o[hh * dec_t:(hh + 1) * dec_t, :], 0.0)
    o_ref[0] = (attn * sz_ref[0].astype(f32)).astype(bf16)


def _attn_kernel(pt_ref, rb_ref, qt_ref, k_ref, vt_ref, bias_ref, szp_ref, q_ref, kn_ref, vn_ref, szs_ref,
                 far_ref, lastb_ref, ownb_ref, *rest, n_pages, head_dim, dec_t, top):
    del pt_ref
    k_refs = rest[:n_pages]
    v_refs = rest[n_pages:2 * n_pages]
    op_ref, os_ref, s_scr, v1_scr, o_scr, p_scr = rest[2 * n_pages:]
    hp, hh = pl.program_id(1), pl.program_id(2)
    heads_per = qt_ref.shape[1] // head_dim

    @pl.when((pl.program_id(0) == 0) & (hp == 0) & (hh == 0))
    def _():
        o_scr[...] = jnp.zeros(o_scr.shape, o_scr.dtype)

    _sample_attention(q_ref, kn_ref, vn_ref, szs_ref, far_ref, lastb_ref, ownb_ref, k_refs, v_refs, os_ref, p_scr,
                      head_dim=head_dim, dec_t=dec_t)
    c_far = rb_ref[top, hp * heads_per + hh] * LOG2E
    _prompt_head_attention(c_far, hh, qt_ref, k_ref, vt_ref, bias_ref, szp_ref, op_ref, s_scr, v1_scr, o_scr,
                           head_dim=head_dim)


def _attention(rel_bias, page_table, qtb, ktok, vtb, bias_tiles, szp, q, kn, vn, szs, far, lastb, ownb,
               cache_k, cache_v, *, layer, batch, seq, head_dim, top):
    n, eb = ktok.shape
    blk = MOBA_BLOCK
    nb = seq // blk
    hw = V7X_LANES
    heads_per = hw // head_dim
    n_hp = eb // hw
    assert heads_per >= 2
    n_dec, dec_t, _ = q.shape
    n_pages = page_table.shape[1]
    ps = cache_k.shape[3]
    ht = (eb // head_dim) * dec_t
    assert n_dec == batch * n_hp * heads_per
    assert MOBA_BLOCK % ps == 0 and (n_pages * ps) % MOBA_BLOCK == 0 and ps == V7X_LANES
    assert ht == V7X_LANES and dec_t & (dec_t - 1) == 0 and dec_t <= ps and n_pages * ps // MOBA_BLOCK <= ht
    sid = lambda b, hp, hh: (b * n_hp + hp) * heads_per + hh
    by_block = lambda a: a.reshape(batch, nb, eb, blk)
    seq_rows = lambda: pl.BlockSpec((seq, hw), lambda b, hp, hh, pt: (b, hp))
    seq_t = lambda: pl.BlockSpec((None, nb, hw, blk), lambda b, hp, hh, pt: (b, 0, hp, 0))
    per_s = lambda: pl.BlockSpec((1, dec_t, eb), lambda b, hp, hh, pt: (sid(b, hp, hh), 0, 0))
    whole = lambda a: pl.BlockSpec(a.shape, lambda b, hp, hh, pt: (0,) * a.ndim)
    page = lambda g: pl.BlockSpec((None, None, eb, ps), lambda b, hp, hh, pt: (layer, pt[sid(b, hp, hh), g], 0, 0))
    grid_spec = pltpu.PrefetchScalarGridSpec(
        num_scalar_prefetch=1,
        grid=(batch, n_hp, heads_per),
        in_specs=[pl.BlockSpec(memory_space=pltpu.SMEM), seq_t(), seq_rows(), seq_t(),
                  pl.BlockSpec((None, 2, blk, blk), lambda b, hp, hh, pt: (hp * heads_per + hh, 0, 0, 0)),
                  seq_rows(), per_s(), per_s(), per_s(), per_s(), whole(far), whole(lastb), whole(ownb)]
                 + [page(g) for g in range(n_pages)] * 2,
        out_specs=[seq_rows(), per_s()],
        scratch_shapes=[pltpu.VMEM((nb * (nb + 1) // 2, blk, blk), f32), pltpu.VMEM((nb, hw, blk), bf16),
                        pltpu.VMEM((heads_per, seq, hw), bf16), pltpu.VMEM((ht, n_pages * ps), f32)],
    )
    return pl.pallas_call(
        functools.partial(_attn_kernel, n_pages=n_pages, head_dim=head_dim, dec_t=dec_t, top=top),
        grid_spec=grid_spec,
        out_shape=[jax.ShapeDtypeStruct((n, eb), bf16), jax.ShapeDtypeStruct((n_dec, dec_t, eb), bf16)],
        compiler_params=_params(3),
        name="attention",
    )(page_table, rel_bias, by_block(qtb), ktok, by_block(vtb), bias_tiles, szp, q, kn, vn, szs, far, lastb, ownb,
      *([cache_k] * n_pages), *([cache_v] * n_pages))


def _out_proj_kernel(x_ref, t_ref, w_ref, y_ref):
    y_ref[...] = x_ref[...] + _dot(t_ref[...], w_ref[...])


def _out_proj(x, t, w):
    n, d = x.shape
    tm = min(OUT_TILE, n)
    assert n % tm == 0
    return pl.pallas_call(
        _out_proj_kernel,
        grid=(n // tm,),
        in_specs=[pl.BlockSpec((tm, d), lambda i: (i, 0)), pl.BlockSpec((tm, t.shape[1]), lambda i: (i, 0)),
                  _resident(w.shape)],
        out_specs=pl.BlockSpec((tm, d), lambda i: (i, 0)),
        out_shape=jax.ShapeDtypeStruct((n, d), f32),
        compiler_params=_params(1),
        name="out_proj",
    )(x, t, w)


def kernel(x_prompt, x_sample, cache_k, cache_v, page_table, g_norm, rel_bias, w_in_a, g_v_a, w_s_a, b_s_a,
           w_out_a, w_in_b, g_q_b, g_k_b, w_out_b):
    batch, seq, d = x_prompt.shape
    n_dec, dec_t, _ = x_sample.shape
    depth = g_norm.shape[0]
    n_buckets, n_heads = rel_bias.shape
    head_dim = g_q_b.shape[1]
    eb = n_heads * head_dim
    e_a = g_v_a.shape[1]
    n_groups = w_s_a.shape[1]
    n_phys, page_size = cache_k.shape[1], cache_k.shape[2]
    past_len = page_table.shape[1] * page_size
    assert seq % MOBA_BLOCK == 0 and past_len % MOBA_BLOCK == 0 and CHUNK % dec_t == 0
    assert head_dim & (head_dim - 1) == 0 and V7X_LANES % head_dim == 0

    breaks, top = _bucket_breaks(n_buckets, 2 * MOBA_BLOCK + dec_t)
    heads_per = V7X_LANES // head_dim
    bias_tiles = pl.pallas_call(
        functools.partial(_prompt_bias_kernel, breaks=breaks, top=top),
        grid=(n_heads,),
        in_specs=[pl.BlockSpec(memory_space=pltpu.SMEM)],
        out_specs=pl.BlockSpec((1, 2, MOBA_BLOCK, MOBA_BLOCK), lambda h: (h, 0, 0, 0)),
        out_shape=jax.ShapeDtypeStruct((n_heads, 2, MOBA_BLOCK, MOBA_BLOCK), f32),
        compiler_params=_params(1),
        name="prompt_bias",
    )(rel_bias)
    ht = n_heads * dec_t
    rbt = jnp.repeat(rel_bias.T, dec_t, axis=0)
    lastb, ownb, far = pl.pallas_call(
        functools.partial(_sample_bias_kernel, breaks=breaks, top=top, dec_t=dec_t),
        out_shape=[jax.ShapeDtypeStruct((ht, MOBA_BLOCK), f32), jax.ShapeDtypeStruct((ht, page_size), f32),
                   jax.ShapeDtypeStruct((ht, V7X_LANES), f32)],
        name="sample_bias",
    )(rbt)

    head_of_lane = jnp.arange(eb, dtype=jnp.int32) // head_dim
    ind = (head_of_lane[:, None] == jnp.arange(V7X_LANES, dtype=jnp.int32)[None, :]).astype(bf16)
    indt = ind.T

    cache_kt = jnp.transpose(cache_k, (0, 1, 3, 4, 2)).reshape(cache_k.shape[0], n_phys, eb, page_size)
    cache_vt = jnp.transpose(cache_v, (0, 1, 3, 4, 2)).reshape(cache_v.shape[0], n_phys, eb, page_size)

    xp = x_prompt.reshape(batch * seq, d)
    xs = x_sample.reshape(n_dec * dec_t, d)
    prompt_kv, k_s, v_s, vch = [], [], [], []
    for i in range(depth):
        j = i // 2
        gn = g_norm[i][None, :]
        if i % 2 == 0:
            w_in = w_in_a[j].astype(bf16)
            w_out = w_out_a[j].astype(bf16)
            gv = g_v_a[j][None, :]
            gd = e_a // n_groups
            b_prompt = jnp.repeat(b_s_a[j].T, gd, axis=1)
            b_sample = jnp.repeat(jnp.tile(b_s_a[j][:, :dec_t], (1, CHUNK // dec_t)).T, gd, axis=1)
            w_sample = jnp.tile(w_s_a[j][:, :dec_t, :dec_t], (1, CHUNK // dec_t, CHUNK // dec_t))
            xp = _layer_a(xp, gn, w_in, gv, w_s_a[j], b_prompt, w_out, sub_len=CHUNK, emit_v=False)
            xs, v_rows = _layer_a(xs, gn, w_in, gv, w_sample, b_sample, w_out, sub_len=dec_t, emit_v=True)
            vch.append(v_rows.reshape(n_dec, dec_t, e_a))
        else:
            w_in = w_in_b[j].astype(bf16)
            w_t = w_in[:, :3 * eb].T
            w_z = w_in[:, 3 * eb:]
            w_out = w_out_b[j].astype(bf16)
            gq = jnp.tile(g_q_b[j], n_heads)[None, :]
            gk = jnp.tile(g_k_b[j], n_heads)[None, :]
            gq_t = jnp.broadcast_to(gq.T, (eb, V7X_LANES))
            gk_t = jnp.broadcast_to(gk.T, (eb, V7X_LANES))
            *prompt_kv, qtb, vtb, ktok, szp = _proj_b_prompt(xp, gn, w_t, w_z, gq_t, gk_t, prompt_kv,
                                                             batch=batch, seq=seq, head_dim=head_dim)
            qs, kns, vns, szs, kts, vts = _proj_b_sample(xs.reshape(n_dec, dec_t * d), gn, w_in, w_t, gq, gk, gk_t,
                                                        ind, indt, dec_t=dec_t, head_dim=head_dim)
            r3 = lambda a: a.reshape(n_dec, dec_t, eb)
            tp, ts = _attention(rel_bias, page_table, qtb, ktok, vtb, bias_tiles, szp,
                                r3(qs), r3(kns), r3(vns), r3(szs), far, lastb, ownb, cache_kt, cache_vt,
                                layer=j, batch=batch, seq=seq, head_dim=head_dim, top=top)
            xp = _out_proj(xp, tp, w_out)
            xs = _out_proj(xs, ts.reshape(n_dec * dec_t, eb), w_out)
            k_s.append(kts)
            v_s.append(vts)
    prompt_out = lambda a: jnp.transpose(a.reshape(a.shape[0], batch, n_heads, head_dim, seq), (0, 1, 4, 2, 3))
    sample_out = lambda parts: jnp.transpose(
        jnp.stack(parts).reshape(len(parts), dec_t, n_heads, head_dim, n_dec), (0, 4, 1, 2, 3))
    return (xp.reshape(batch, seq, d), xs.reshape(n_dec, dec_t, d), prompt_out(prompt_kv[0]),
            prompt_out(prompt_kv[1]), sample_out(k_s), sample_out(v_s), jnp.stack(vch))
```

```python
import functools
import math

import numpy as np
import jax
import jax.numpy as jnp
from jax import lax
from jax.experimental import pallas as pl
from jax.experimental.pallas import tpu as pltpu

EPS = 1e-6
CHUNK = 128
MOBA_BLOCK = 256
MOBA_TOPK = 3
MAX_DISTANCE = 128
NEG = -1e30
LOG2E = math.log2(math.e)
V7X_LANES = 128
V7X_SUBLANES = 8
V7X_MXU_DEPTH = 256
V7X_VMEM_LIMIT_BYTES = 56 * 1024 * 1024
TOKEN_TILE = 256
OUT_TILE = 512

bf16 = jnp.bfloat16
f32 = jnp.float32


def _dot(a, b):
    return jnp.dot(a, b, preferred_element_type=f32)


def _dot_nt(a, b):
    return lax.dot_general(a, b, (((1,), (1,)), ((), ())), preferred_element_type=f32)


def _gelu(x):
    return 0.5 * x * (1.0 + lax.erf(x * (2.0 ** -0.5)))


def _silu(x):
    return x * jax.nn.sigmoid(x)


def _rms_rows(x, g):
    return x * lax.rsqrt(jnp.mean(x * x, axis=-1, keepdims=True) + EPS) * g


def _params(n_axes):
    return pltpu.CompilerParams(dimension_semantics=("arbitrary",) * n_axes,
                                vmem_limit_bytes=V7X_VMEM_LIMIT_BYTES)


def _resident(shape):
    nd = len(shape)
    return pl.BlockSpec(shape, lambda *_: (0,) * nd, pipeline_mode=pl.Buffered(1))


def _div_static(x, n):
    if n & (n - 1) == 0:
        return x >> (n.bit_length() - 1)
    return lax.div(x, n)


def _rank_rows(gate, n_valid):
    bid = lax.broadcasted_iota(jnp.int32, (gate.shape[0], 1), 0)
    rank = jnp.zeros(gate.shape, jnp.int32)
    for jp in range(n_valid):
        rowv = gate[jp:jp + 1, :]
        beats = (rowv > gate) | ((rowv == gate) & (jp < bid))
        rank = rank + jnp.where(beats, 1, 0)
    return rank, bid


def _bucket_breaks(n_buckets, max_n):
    n = np.arange(max_n + 1, dtype=np.int32)
    max_exact = n_buckets // 2
    nf = np.maximum(n, 1).astype(np.float32)
    large = max_exact + (np.log(nf / np.float32(max_exact)) / np.float32(math.log(MAX_DISTANCE / max_exact))
                         * np.float32(n_buckets - max_exact)).astype(np.int32)
    b = np.where(n < max_exact, n, np.minimum(large, n_buckets - 1))
    assert np.all(np.diff(b) >= 0)
    breaks = []
    for v in np.unique(b)[:-1]:
        breaks.append((int(v), int(np.max(n[b == v]))))
    top = int(b[-1])
    assert int(b[MOBA_BLOCK + 1]) == top
    return breaks, top


def _bias_of_dist(n, value_of_bucket, breaks, top):
    val = jnp.where(n <= breaks[-1][1], value_of_bucket(breaks[-1][0]), value_of_bucket(top))
    for bv, last in reversed(breaks[:-1]):
        val = jnp.where(n <= last, value_of_bucket(bv), val)
    return val


def _prompt_bias_kernel(rb_ref, o_ref, *, breaks, top):
    h = pl.program_id(0)
    key = lax.broadcasted_iota(jnp.int32, (MOBA_BLOCK, MOBA_BLOCK), 0)
    qry = lax.broadcasted_iota(jnp.int32, (MOBA_BLOCK, MOBA_BLOCK), 1)
    for kind in (0, 1):
        n = qry - key + kind * MOBA_BLOCK
        val = _bias_of_dist(n, lambda b: rb_ref[b, h], breaks, top) * LOG2E
        if kind == 0:
            val = jnp.where(n < 0, NEG, val)
        o_ref[0, kind] = val.astype(f32)


def _sample_bias_kernel(rbt_ref, last_ref, own_ref, far_ref, *, breaks, top, dec_t):
    ht = rbt_ref.shape[0]
    t = lax.broadcasted_iota(jnp.int32, (ht, 1), 0) & (dec_t - 1)
    col = lambda b: rbt_ref[:, b:b + 1]
    c = lax.broadcasted_iota(jnp.int32, (1, last_ref.shape[1]), 1)
    last_ref[...] = _bias_of_dist(MOBA_BLOCK + t - c, col, breaks, top)
    c = lax.broadcasted_iota(jnp.int32, (1, own_ref.shape[1]), 1)
    n = t - c
    own_ref[...] = jnp.where(n < 0, NEG, _bias_of_dist(n, col, breaks, top))
    far_ref[...] = jnp.broadcast_to(col(top), far_ref.shape)


def _layer_a_kernel(x_ref, gn_ref, win_ref, gv_ref, wmix_ref, bmix_ref, wout_ref, y_ref, *rest,
                    sub_len, emit_v):
    if emit_v:
        v_ref, vn_scr = rest
    else:
        (vn_scr,) = rest
    tm = x_ref.shape[0]
    e = gv_ref.shape[1]
    n_groups = wmix_ref.shape[0]
    gd = e // n_groups
    n_chunks = tm // CHUNK
    wide = 2 * V7X_LANES
    g_per = wide // gd

    x = x_ref[...]
    h = _rms_rows(x, gn_ref[...]).astype(bf16)

    vg = _gelu(_dot(h, win_ref[:, e:2 * e]))
    vn = _rms_rows(vg, gv_ref[...])
    if emit_v:
        v_ref[...] = vn
    vn_scr[...] = vn.astype(bf16)

    r = lax.broadcasted_iota(jnp.int32, (CHUNK, CHUNK), 0)
    c = lax.broadcasted_iota(jnp.int32, (CHUNK, CHUNK), 1)
    keep = r >= c
    if sub_len != CHUNK:
        keep = keep & ((r ^ c) < sub_len)
    bias_rows = [bmix_ref[...]] * n_chunks

    acc = jnp.zeros(y_ref.shape, f32)
    for p in range(e // wide):
        lo = p * wide
        u = _gelu(_dot(h, win_ref[:, lo:lo + wide]))
        z = _dot(h, win_ref[:, 2 * e + lo:2 * e + lo + wide])
        parts = []
        for gg in range(g_per):
            g = p * g_per + gg
            wm = jnp.where(keep, wmix_ref[g], 0.0).astype(bf16)
            rhs = jnp.concatenate(
                [vn_scr[k * CHUNK:(k + 1) * CHUNK, g * gd:(g + 1) * gd] for k in range(n_chunks)], axis=1)
            sg = _dot(wm, rhs)
            parts.append(jnp.concatenate([sg[:, k * gd:(k + 1) * gd] for k in range(n_chunks)], axis=0))
        s = jnp.concatenate(parts, axis=1) + jnp.concatenate(
            [b[:, lo:lo + wide] for b in bias_rows], axis=0)
        t = (u * s * _silu(z)).astype(bf16)
        acc = acc + _dot(t, wout_ref[lo:lo + wide, :])
    y_ref[...] = x + acc


def _layer_a(x, gn, w_in, g_v, w_mix, b_mix, w_out, *, sub_len, emit_v):
    n, d = x.shape
    e = g_v.shape[1]
    tm = TOKEN_TILE
    assert n % tm == 0 and tm % CHUNK == 0 and e % (2 * V7X_LANES) == 0
    row_spec = lambda width: pl.BlockSpec((tm, width), lambda i: (i, 0))
    out_shape = [jax.ShapeDtypeStruct((n, d), f32)]
    out_specs = [row_spec(d)]
    if emit_v:
        out_shape.append(jax.ShapeDtypeStruct((n, e), f32))
        out_specs.append(row_spec(e))
    res = pl.pallas_call(
        functools.partial(_layer_a_kernel, sub_len=sub_len, emit_v=emit_v),
        grid=(n // tm,),
        in_specs=[row_spec(d), _resident(gn.shape), _resident(w_in.shape), _resident(g_v.shape),
                  _resident(w_mix.shape), _resident(b_mix.shape), _resident(w_out.shape)],
        out_specs=out_specs,
        out_shape=out_shape,
        scratch_shapes=[pltpu.VMEM((tm, e), bf16)],
        compiler_params=_params(1),
        name="layer_a_sample" if emit_v else "layer_a_prompt",
    )(x, gn, w_in, g_v, w_mix, b_mix, w_out)
    return res if emit_v else res[0]


def _head_norm_t(a, g_ref, head_dim):
    eb, tm = a.shape
    a3 = a.reshape(eb // head_dim, head_dim, tm)
    rs = lax.rsqrt(jnp.mean(a3 * a3, axis=1, keepdims=True) + EPS)
    g = jnp.concatenate([g_ref[...]] * (tm // V7X_LANES), axis=1).reshape(a3.shape)
    return (a3 * rs * g).reshape(eb, tm)


def _head_norm_rows(a, g_ref, ind_ref, indt_ref, head_dim):
    ss = _dot((a * a).astype(bf16), ind_ref[...])
    rs = lax.rsqrt(ss * (1.0 / head_dim) + EPS)
    rs_hi = rs.astype(bf16)
    rs_lo = (rs - rs_hi.astype(f32)).astype(bf16)
    scale = _dot(rs_hi, indt_ref[...]) + _dot(rs_lo, indt_ref[...])
    return a * scale * g_ref[...]


def _proj_b_prompt_kernel(x_ref, gn_ref, wt_ref, wz_ref, gq_ref, gk_ref, *rest, head_dim, n_prev):
    prev = rest[:2 if n_prev else 0]
    kt_ref, vt_ref, qtb_ref, vtb_ref, ktok_ref, sz_ref, ksum_ref = rest[len(prev):]
    eb = wz_ref.shape[1]
    h = _rms_rows(x_ref[...], gn_ref[...]).astype(bf16)
    qt = _head_norm_t(_dot_nt(wt_ref[0:eb, :], h), gq_ref, head_dim) * (head_dim ** -0.5 * LOG2E)
    qtb_ref[...] = qt.astype(bf16)
    kt = _head_norm_t(_dot_nt(wt_ref[eb:2 * eb, :], h), gk_ref, head_dim)
    ktok = kt.T
    ktok_ref[...] = ktok.astype(bf16)
    ksum_ref[0] = jnp.sum(ktok, axis=0, keepdims=True)
    vt = _dot_nt(wt_ref[2 * eb:3 * eb, :], h)
    vtb_ref[...] = vt.astype(bf16)
    sz_ref[...] = _silu(_dot(h, wz_ref[...])).astype(bf16)
    if n_prev:
        kt_ref[0:n_prev] = prev[0][...]
        vt_ref[0:n_prev] = prev[1][...]
    kt_ref[n_prev] = kt
    vt_ref[n_prev] = vt


def _proj_b_prompt(x, gn, w_t, w_z, gq_t, gk_t, prev_kv, *, batch, seq, head_dim):
    n, d = x.shape
    eb = w_z.shape[1]
    tm = TOKEN_TILE
    assert seq % tm == 0 and tm == MOBA_BLOCK
    nt = seq // tm
    n_prev = prev_kv[0].shape[0] if prev_kv else 0
    earlier = lambda: pl.BlockSpec((n_prev, None, eb, tm), lambda i: (0, i // nt, 0, i % nt))
    final_t = lambda: pl.BlockSpec((n_prev + 1, None, eb, tm), lambda i: (0, i // nt, 0, i % nt))
    tile_t = lambda: pl.BlockSpec((None, eb, tm), lambda i: (i, 0, 0))
    rows = lambda: pl.BlockSpec((tm, eb), lambda i: (i, 0))
    stacked = jax.ShapeDtypeStruct((n_prev + 1, batch, eb, seq), f32)
    return pl.pallas_call(
        functools.partial(_proj_b_prompt_kernel, head_dim=head_dim, n_prev=n_prev),
        grid=(n // tm,),
        in_specs=[pl.BlockSpec((tm, d), lambda i: (i, 0)), _resident(gn.shape), _resident(w_t.shape),
                  _resident(w_z.shape), _resident(gq_t.shape), _resident(gk_t.shape)]
                 + [earlier() for _ in prev_kv],
        out_specs=[final_t(), final_t(), tile_t(), tile_t(), rows(), rows(),
                   pl.BlockSpec((1, 1, eb), lambda i: (i, 0, 0))],
        out_shape=[stacked, stacked,
                   jax.ShapeDtypeStruct((n // tm, eb, tm), bf16), jax.ShapeDtypeStruct((n // tm, eb, tm), bf16),
                   jax.ShapeDtypeStruct((n, eb), bf16), jax.ShapeDtypeStruct((n, eb), bf16),
                   jax.ShapeDtypeStruct((n // tm, 1, eb), f32)],
        compiler_params=_params(1),
        name="proj_b_prompt",
    )(x, gn, w_t, w_z, gq_t, gk_t, *prev_kv)


def _proj_b_sample_kernel(x_ref, gn_ref, w_ref, wt_ref, gq_ref, gk_ref, gkt_ref, ind_ref, indt_ref,
                          q_ref, kn_ref, vn_ref, sz_ref, kt_ref, vt_ref, *, head_dim):
    nseq = x_ref.shape[0]
    eb = gq_ref.shape[1]
    d = gn_ref.shape[1]
    t_per = x_ref.shape[1] // d
    x = jnp.concatenate([x_ref[:, t * d:(t + 1) * d] for t in range(t_per)], axis=0)
    h = _rms_rows(x, gn_ref[...]).astype(bf16)

    def put_rows(ref, a):
        for t in range(t_per):
            ref[:, t * eb:(t + 1) * eb] = a[t * nseq:(t + 1) * nseq].astype(ref.dtype)

    def put_t(ref, a):
        for t in range(t_per):
            ref[t] = a[:, t * nseq:(t + 1) * nseq]

    q = _head_norm_rows(_dot(h, w_ref[:, 0:eb]), gq_ref, ind_ref, indt_ref, head_dim) * (head_dim ** -0.5)
    put_rows(q_ref, q)
    put_rows(kn_ref, _head_norm_rows(_dot(h, w_ref[:, eb:2 * eb]), gk_ref, ind_ref, indt_ref, head_dim))
    put_rows(vn_ref, _dot(h, w_ref[:, 2 * eb:3 * eb]))
    put_rows(sz_ref, _silu(_dot(h, w_ref[:, 3 * eb:4 * eb])))
    put_t(kt_ref, _head_norm_t(_dot_nt(wt_ref[eb:2 * eb, :], h), gkt_ref, head_dim))
    put_t(vt_ref, _dot_nt(wt_ref[2 * eb:3 * eb, :], h))


def _proj_b_sample(x, gn, w, w_t, gq, gk, gk_t, ind, indt, *, dec_t, head_dim):
    nseq = x.shape[0]
    d = gn.shape[1]
    eb = gq.shape[1]
    assert nseq == V7X_LANES and TOKEN_TILE % nseq == 0
    t_per = TOKEN_TILE // nseq
    assert dec_t % t_per == 0
    rows = lambda: pl.BlockSpec((nseq, t_per * eb), lambda i: (0, i))
    sds = lambda dt: jax.ShapeDtypeStruct((nseq, dec_t * eb), dt)
    tsp = lambda: pl.BlockSpec((t_per, eb, nseq), lambda i: (i, 0, 0))
    tsd = jax.ShapeDtypeStruct((dec_t, eb, nseq), f32)
    return pl.pallas_call(
        functools.partial(_proj_b_sample_kernel, head_dim=head_dim),
        grid=(dec_t // t_per,),
        in_specs=[pl.BlockSpec((nseq, t_per * d), lambda i: (0, i)), _resident(gn.shape), _resident(w.shape),
                  _resident(w_t.shape), _resident(gq.shape), _resident(gk.shape), _resident(gk_t.shape),
                  _resident(ind.shape), _resident(indt.shape)],
        out_specs=[rows(), rows(), rows(), rows(), tsp(), tsp()],
        out_shape=[sds(f32), sds(f32), sds(f32), sds(f32), tsd, tsd],
        compiler_params=_params(1),
        name="proj_b_sample",
    )(x, gn, w, w_t, gq, gk, gk_t, ind, indt)


def _prompt_head_attention(c_far, hh, qt_ref, k_ref, ksum_ref, vt_ref, bias_ref, sz_ref, o_ref, s_scr, v1_scr, o_scr,
                           *, head_dim, between=None):
    blk = MOBA_BLOCK
    nb, hw, _ = qt_ref.shape
    heads_per = hw // head_dim
    row = lax.broadcasted_iota(jnp.int32, (hw, 1), 0)
    in_head = (row >= hh * head_dim) & (row < (hh + 1) * head_dim)
    lane = lax.broadcasted_iota(jnp.int32, (1, hw), 1)
    lane_in_head = (lane >= hh * head_dim) & (lane < (hh + 1) * head_dim)
    for j in range(nb):
        v1_scr[j] = jnp.where(in_head, vt_ref[j], jnp.ones((hw, blk), bf16))
    nr = -(-nb // 8) * 8
    ksum = jnp.concatenate([ksum_ref[...]] + [jnp.zeros((nr - nb, hw), f32)] * (nr > nb), axis=0)
    ksum_hi = ksum.astype(bf16)
    ksum_hl = jnp.concatenate([ksum_hi, (ksum - ksum_hi.astype(f32)).astype(bf16)], axis=0)
    pair = lambda i, j: i * (i + 1) // 2 + j
    shifts = {}
    for i in range(nb):
        qm = jnp.where(in_head, qt_ref[i], jnp.zeros((hw, blk), bf16))
        maxes = []
        for j in range(i + 1):
            s = _dot(k_ref[j * blk:(j + 1) * blk, :], qm)
            if j == i:
                s = s + bias_ref[0]
            elif j == i - 1:
                s = s + bias_ref[1]
            s_scr[pair(i, j)] = s
            mx = jnp.max(s, axis=0, keepdims=True)
            maxes.append(mx + c_far if j < i - 1 else mx)
        chosen = [None] * i
        if i > MOBA_TOPK:
            g2 = _dot(ksum_hl, qm)
            rank, _ = _rank_rows(g2[0:nr] + g2[nr:2 * nr], i)
            chosen = [rank[j:j + 1, :] < MOBA_TOPK for j in range(i)]
        m = maxes[i]
        for j in range(i):
            m = jnp.maximum(m, maxes[j] if chosen[j] is None else jnp.where(chosen[j], maxes[j], NEG))
        for j in range(i + 1):
            shift = m - c_far if j < i - 1 else m
            if j < i and chosen[j] is not None:
                shift = jnp.where(chosen[j], shift, -NEG)
            shifts[i, j] = shift
    if between is not None:
        between()
    for i in range(nb):
        acc = jnp.zeros((hw, blk), f32)
        for j in range(i + 1):
            acc = acc + _dot(v1_scr[j], jnp.exp2(s_scr[pair(i, j)] - shifts[i, j]).astype(bf16))
        den = jnp.max(jnp.where(in_head, 0.0, acc), axis=0, keepdims=True)
        rows = slice(i * blk, (i + 1) * blk)
        t = ((acc / den).T * sz_ref[rows, :].astype(f32)).astype(bf16)
        o_scr[hh, rows, :] = jnp.where(lane_in_head, t, jnp.zeros_like(t))
    total = o_scr[0]
    for other in range(1, heads_per):
        total = total + o_scr[other]
    o_ref[...] = total


def _sample_attention(seq_row, q_ref, kn_ref, vn_ref, sz_ref, far_ref, lastb_ref, ownb_ref, k_refs, v_refs, o_ref,
                      p_scr, *, head_dim, dec_t):
    eb, ps = k_refs[0].shape
    ht = p_scr.shape[0]
    gw = V7X_MXU_DEPTH
    g_heads = gw // head_dim
    gr = g_heads * dec_t
    n_grp = eb // gw
    ppb = MOBA_BLOCK // ps
    nblk = len(k_refs) // ppb
    pad_rows = lambda a: jnp.concatenate([a, jnp.zeros((ps - dec_t, eb), f32)], axis=0).astype(bf16)
    block = lambda refs, j: jnp.concatenate([refs[j * ppb + g][...].astype(bf16) for g in range(ppb)], axis=1)
    cols = lambda j: slice(j * MOBA_BLOCK, (j + 1) * MOBA_BLOCK)
    lane_tiles = lambda a: [a[:, q * V7X_LANES:(q + 1) * V7X_LANES] for q in range(a.shape[1] // V7X_LANES)]
    grp_rows = lambda g: slice(g * gr, (g + 1) * gr)
    grp_lanes = lambda g: slice(g * gw, (g + 1) * gw)

    def tokens(ref):
        row = ref[pl.ds(seq_row, 1), :]
        return jnp.concatenate([row[:, t * eb:(t + 1) * eb] for t in range(dec_t)], axis=0)

    lane_head = lax.broadcasted_iota(jnp.int32, (1, gw), 1) >> (head_dim.bit_length() - 1)
    row_head = lax.broadcasted_iota(jnp.int32, (gr, 1), 0) >> (dec_t.bit_length() - 1)
    q = tokens(q_ref)
    qbd = [jnp.where(row_head == lane_head, jnp.concatenate([q[:, grp_lanes(g)]] * g_heads, axis=0), 0.0).astype(bf16)
           for g in range(n_grp)]

    lane = lax.broadcasted_iota(jnp.int32, (1, V7X_LANES), 1)
    gm = jnp.zeros((ht, V7X_LANES), f32)
    for j in range(nblk):
        kb = block(k_refs, j)
        sj = jnp.concatenate([_dot(qbd[g], kb[grp_lanes(g), :]) for g in range(n_grp)], axis=0)
        p_scr[:, cols(j)] = sj
        gm = jnp.where(lane == j, jnp.sum(sj, axis=1, keepdims=True), gm)
    nrow = -(-nblk // 8) * 8
    rank, bid = _rank_rows(gm.T[0:nrow, :], nblk)
    sel = jnp.where((bid < nblk) & (rank < MOBA_TOPK), 1.0, 0.0)
    selm = jnp.concatenate([sel, jnp.zeros((V7X_LANES - nrow, ht), f32)], axis=0).T

    kn = pad_rows(tokens(kn_ref))
    s_own = jnp.concatenate([_dot_nt(qbd[g], kn[:, grp_lanes(g)]) for g in range(n_grp)], axis=0) + ownb_ref[...]
    far = jnp.concatenate([far_ref[...]] * (MOBA_BLOCK // V7X_LANES), axis=1)
    mx = s_own
    for j in range(nblk):
        sj = p_scr[:, cols(j)] + (lastb_ref[...] if j == nblk - 1 else far)
        sj = jnp.where(selm[:, j:j + 1] > 0.5, sj, NEG)
        p_scr[:, cols(j)] = sj
        for tile in lane_tiles(sj):
            mx = jnp.maximum(mx, tile)
    mx = jnp.max(mx, axis=1, keepdims=True)
    p_own = jnp.exp(s_own - mx)
    lsum = p_own
    vn = pad_rows(tokens(vn_ref))
    acc = [_dot(p_own[grp_rows(g)].astype(bf16), vn[:, grp_lanes(g)]) for g in range(n_grp)]
    for j in range(nblk):
        pj = jnp.exp(p_scr[:, cols(j)] - mx)
        for tile in lane_tiles(pj):
            lsum = lsum + tile
        pj = pj.astype(bf16)
        vb = block(v_refs, j)
        acc = [acc[g] + _dot_nt(pj[grp_rows(g)], vb[grp_lanes(g), :]) for g in range(n_grp)]
    inv = 1.0 / jnp.sum(lsum, axis=1, keepdims=True)
    parts = []
    for g in range(n_grp):
        o = acc[g] * inv[grp_rows(g)]
        part = jnp.zeros((dec_t, gw), f32)
        for hl in range(g_heads):
            part = part + jnp.where(lane_head == hl, o[hl * dec_t:(hl + 1) * dec_t, :], 0.0)
        parts.append(part)
    o_ref[0] = (jnp.concatenate(parts, axis=1) * tokens(sz_ref)).astype(bf16)


def _attn_kernel(pt_ref, rb_ref, qt_ref, k_ref, ksum_ref, vt_ref, bias_ref, szp_ref, q_ref, kn_ref, vn_ref, szs_ref,
                 far_ref, lastb_ref, ownb_ref, *rest, n_pages, head_dim, dec_t, top):
    del pt_ref
    k_refs = rest[:n_pages]
    v_refs = rest[n_pages:2 * n_pages]
    op_ref, os_ref, s_scr, v1_scr, o_scr, p_scr = rest[2 * n_pages:]
    step = pl.program_id(0)
    heads_per = qt_ref.shape[1] // head_dim
    n_heads = rb_ref.shape[1]
    head = step - _div_static(step, n_heads) * n_heads
    hh = head - _div_static(head, heads_per) * heads_per

    @pl.when(step == 0)
    def _():
        o_scr[...] = jnp.zeros(o_scr.shape, o_scr.dtype)

    seq_group = q_ref.shape[0]
    seq_row = step - _div_static(step, seq_group) * seq_group
    sample = functools.partial(_sample_attention, seq_row, q_ref, kn_ref, vn_ref, szs_ref, far_ref, lastb_ref,
                               ownb_ref, k_refs, v_refs, os_ref, p_scr, head_dim=head_dim, dec_t=dec_t)
    c_far = rb_ref[top, head] * LOG2E
    _prompt_head_attention(c_far, hh, qt_ref, k_ref, ksum_ref, vt_ref, bias_ref, szp_ref, op_ref, s_scr, v1_scr,
                           o_scr, head_dim=head_dim, between=sample)


def _attention(rel_bias, page_table, qtb, ktok, ksum, vtb, bias_tiles, szp, q, kn, vn, szs, far, lastb, ownb,
               cache_k, cache_v, *, layer, batch, seq, head_dim, top):
    n, eb = ktok.shape
    blk = MOBA_BLOCK
    nb = seq // blk
    hw = V7X_LANES
    heads_per = hw // head_dim
    n_hp = eb // hw
    assert heads_per >= 2
    n_dec = q.shape[0]
    dec_t = q.shape[1] // eb
    n_pages = page_table.shape[1]
    ps = cache_k.shape[3]
    ht = (eb // head_dim) * dec_t
    assert n_dec == batch * n_hp * heads_per
    assert MOBA_BLOCK % ps == 0 and (n_pages * ps) % MOBA_BLOCK == 0 and ps == V7X_LANES
    assert ht == V7X_LANES and dec_t & (dec_t - 1) == 0 and dec_t <= ps and n_pages * ps // MOBA_BLOCK <= ht
    assert eb % V7X_MXU_DEPTH == 0 and V7X_MXU_DEPTH % head_dim == 0 and n_dec % V7X_SUBLANES == 0
    b_of = lambda s: _div_static(s, n_hp * heads_per)
    hp_of = lambda s: _div_static(s, heads_per) - b_of(s) * n_hp
    head_of = lambda s: s - b_of(s) * (n_hp * heads_per)
    by_block = lambda a: a.reshape(batch, nb, eb, blk)
    seq_rows = lambda: pl.BlockSpec((seq, hw), lambda s, pt: (b_of(s), hp_of(s)))
    seq_t = lambda: pl.BlockSpec((None, nb, hw, blk), lambda s, pt: (b_of(s), 0, hp_of(s), 0))
    grp = V7X_SUBLANES
    per_grp = lambda: pl.BlockSpec((grp, dec_t * eb), lambda s, pt: (_div_static(s, grp), 0))
    per_s = lambda: pl.BlockSpec((1, dec_t, eb), lambda s, pt: (s, 0, 0))
    whole = lambda a: pl.BlockSpec(a.shape, lambda s, pt: (0,) * a.ndim)
    page = lambda g: pl.BlockSpec((None, None, eb, ps), lambda s, pt: (layer, pt[s * n_pages + g], 0, 0))
    grid_spec = pltpu.PrefetchScalarGridSpec(
        num_scalar_prefetch=1,
        grid=(n_dec,),
        in_specs=[pl.BlockSpec(memory_space=pltpu.SMEM), seq_t(), seq_rows(),
                  pl.BlockSpec((None, nb, hw), lambda s, pt: (b_of(s), 0, hp_of(s))), seq_t(),
                  pl.BlockSpec((None, 2, blk, blk), lambda s, pt: (head_of(s), 0, 0, 0)),
                  seq_rows(), per_grp(), per_grp(), per_grp(), per_grp(), whole(far), whole(lastb), whole(ownb)]
                 + [page(g) for g in range(n_pages)] * 2,
        out_specs=[seq_rows(), per_s()],
        scratch_shapes=[pltpu.VMEM((nb * (nb + 1) // 2, blk, blk), f32), pltpu.VMEM((nb, hw, blk), bf16),
                        pltpu.VMEM((heads_per, seq, hw), bf16), pltpu.VMEM((ht, n_pages * ps), f32)],
    )
    return pl.pallas_call(
        functools.partial(_attn_kernel, n_pages=n_pages, head_dim=head_dim, dec_t=dec_t, top=top),
        grid_spec=grid_spec,
        out_shape=[jax.ShapeDtypeStruct((n, eb), bf16), jax.ShapeDtypeStruct((n_dec, dec_t, eb), bf16)],
        compiler_params=_params(1),
        name="attention",
    )(page_table.reshape(-1), rel_bias, by_block(qtb), ktok, ksum.reshape(batch, nb, eb), by_block(vtb), bias_tiles, szp,
      q, kn, vn, szs, far, lastb, ownb,
      *([cache_k] * n_pages), *([cache_v] * n_pages))


def _out_proj_kernel(x_ref, t_ref, w_ref, y_ref):
    y_ref[...] = x_ref[...] + _dot(t_ref[...], w_ref[...])


def _out_proj(x, t, w):
    n, d = x.shape
    tm = min(OUT_TILE, n)
    assert n % tm == 0
    return pl.pallas_call(
        _out_proj_kernel,
        grid=(n // tm,),
        in_specs=[pl.BlockSpec((tm, d), lambda i: (i, 0)), pl.BlockSpec((tm, t.shape[1]), lambda i: (i, 0)),
                  _resident(w.shape)],
        out_specs=pl.BlockSpec((tm, d), lambda i: (i, 0)),
        out_shape=jax.ShapeDtypeStruct((n, d), f32),
        compiler_params=_params(1),
        name="out_proj",
    )(x, t, w)


def kernel(x_prompt, x_sample, cache_k, cache_v, page_table, g_norm, rel_bias, w_in_a, g_v_a, w_s_a, b_s_a,
           w_out_a, w_in_b, g_q_b, g_k_b, w_out_b):
    batch, seq, d = x_prompt.shape
    n_dec, dec_t, _ = x_sample.shape
    depth = g_norm.shape[0]
    n_buckets, n_heads = rel_bias.shape
    head_dim = g_q_b.shape[1]
    eb = n_heads * head_dim
    e_a = g_v_a.shape[1]
    n_groups = w_s_a.shape[1]
    n_phys, page_size = cache_k.shape[1], cache_k.shape[2]
    past_len = page_table.shape[1] * page_size
    assert seq % MOBA_BLOCK == 0 and past_len % MOBA_BLOCK == 0 and CHUNK % dec_t == 0
    assert head_dim & (head_dim - 1) == 0 and V7X_LANES % head_dim == 0

    breaks, top = _bucket_breaks(n_buckets, 2 * MOBA_BLOCK + dec_t)
    heads_per = V7X_LANES // head_dim
    bias_tiles = pl.pallas_call(
        functools.partial(_prompt_bias_kernel, breaks=breaks, top=top),
        grid=(n_heads,),
        in_specs=[pl.BlockSpec(memory_space=pltpu.SMEM)],
        out_specs=pl.BlockSpec((1, 2, MOBA_BLOCK, MOBA_BLOCK), lambda h: (h, 0, 0, 0)),
        out_shape=jax.ShapeDtypeStruct((n_heads, 2, MOBA_BLOCK, MOBA_BLOCK), f32),
        compiler_params=_params(1),
        name="prompt_bias",
    )(rel_bias)
    ht = n_heads * dec_t
    rbt = jnp.repeat(rel_bias.T, dec_t, axis=0)
    lastb, ownb, far = pl.pallas_call(
        functools.partial(_sample_bias_kernel, breaks=breaks, top=top, dec_t=dec_t),
        out_shape=[jax.ShapeDtypeStruct((ht, MOBA_BLOCK), f32), jax.ShapeDtypeStruct((ht, page_size), f32),
                   jax.ShapeDtypeStruct((ht, V7X_LANES), f32)],
        name="sample_bias",
    )(rbt)

    head_of_lane = jnp.arange(eb, dtype=jnp.int32) // head_dim
    ind = (head_of_lane[:, None] == jnp.arange(V7X_LANES, dtype=jnp.int32)[None, :]).astype(bf16)
    indt = ind.T

    cache_kt = jnp.transpose(cache_k, (0, 1, 3, 4, 2)).reshape(cache_k.shape[0], n_phys, eb, page_size)
    cache_vt = jnp.transpose(cache_v, (0, 1, 3, 4, 2)).reshape(cache_v.shape[0], n_phys, eb, page_size)

    xp = x_prompt.reshape(batch * seq, d)
    xs = x_sample.reshape(n_dec * dec_t, d)
    prompt_kv, k_s, v_s, vch = [], [], [], []
    for i in range(depth):
        j = i // 2
        gn = g_norm[i][None, :]
        if i % 2 == 0:
            w_in = w_in_a[j].astype(bf16)
            w_out = w_out_a[j].astype(bf16)
            gv = g_v_a[j][None, :]
            gd = e_a // n_groups
            b_prompt = jnp.repeat(b_s_a[j].T, gd, axis=1)
            b_sample = jnp.repeat(jnp.tile(b_s_a[j][:, :dec_t], (1, CHUNK // dec_t)).T, gd, axis=1)
            w_sample = jnp.tile(w_s_a[j][:, :dec_t, :dec_t], (1, CHUNK // dec_t, CHUNK // dec_t))
            xp = _layer_a(xp, gn, w_in, gv, w_s_a[j], b_prompt, w_out, sub_len=CHUNK, emit_v=False)
            xs, v_rows = _layer_a(xs, gn, w_in, gv, w_sample, b_sample, w_out, sub_len=dec_t, emit_v=True)
            vch.append(v_rows.reshape(n_dec, dec_t, e_a))
        else:
            w_in = w_in_b[j].astype(bf16)
            w_t = w_in[:, :3 * eb].T
            w_z = w_in[:, 3 * eb:]
            w_out = w_out_b[j].astype(bf16)
            gq = jnp.tile(g_q_b[j], n_heads)[None, :]
            gk = jnp.tile(g_k_b[j], n_heads)[None, :]
            gq_t = jnp.broadcast_to(gq.T, (eb, V7X_LANES))
            gk_t = jnp.broadcast_to(gk.T, (eb, V7X_LANES))
            *prompt_kv, qtb, vtb, ktok, szp, ksum = _proj_b_prompt(xp, gn, w_t, w_z, gq_t, gk_t, prompt_kv,
                                                                   batch=batch, seq=seq, head_dim=head_dim)
            qs, kns, vns, szs, kts, vts = _proj_b_sample(xs.reshape(n_dec, dec_t * d), gn, w_in, w_t, gq, gk, gk_t,
                                                        ind, indt, dec_t=dec_t, head_dim=head_dim)
            tp, ts = _attention(rel_bias, page_table, qtb, ktok, ksum, vtb, bias_tiles, szp,
                                qs, kns, vns, szs, far, lastb, ownb, cache_kt, cache_vt,
                                layer=j, batch=batch, seq=seq, head_dim=head_dim, top=top)
            xp = _out_proj(xp, tp, w_out)
            xs = _out_proj(xs, ts.reshape(n_dec * dec_t, eb), w_out)
            k_s.append(kts)
            v_s.append(vts)
    prompt_out = lambda a: jnp.transpose(a.reshape(a.shape[0], batch, n_heads, head_dim, seq), (0, 1, 4, 2, 3))
    sample_out = lambda parts: jnp.transpose(
        jnp.stack(parts).reshape(len(parts), dec_t, n_heads, head_dim, n_dec), (0, 4, 1, 2, 3))
    return (xp.reshape(batch, seq, d), xs.reshape(n_dec, dec_t, d), prompt_out(prompt_kv[0]),
            prompt_out(prompt_kv[1]), sample_out(k_s), sample_out(v_s), jnp.stack(vch))
```

```python
import functools
import math

import numpy as np
import jax
import jax.numpy as jnp
from jax import lax
from jax.experimental import pallas as pl
from jax.experimental.pallas import tpu as pltpu

EPS = 1e-6
CHUNK = 128
MOBA_BLOCK = 256
MOBA_TOPK = 3
MAX_DISTANCE = 128
NEG = -1e30
LOG2E = math.log2(math.e)
V7X_LANES = 128
V7X_SUBLANES = 8
V7X_MXU_DEPTH = 256
V7X_VMEM_LIMIT_BYTES = 56 * 1024 * 1024
TOKEN_TILE = 256
OUT_TILE = 512
KEY_ROWS = 128

bf16 = jnp.bfloat16
f32 = jnp.float32


def _dot(a, b):
    return jnp.dot(a, b, preferred_element_type=f32)


def _dot_nt(a, b):
    return lax.dot_general(a, b, (((1,), (1,)), ((), ())), preferred_element_type=f32)


def _gelu(x):
    return 0.5 * x * (1.0 + lax.erf(x * (2.0 ** -0.5)))


def _silu(x):
    return x * jax.nn.sigmoid(x)


def _rms_rows(x, g):
    return x * lax.rsqrt(jnp.mean(x * x, axis=-1, keepdims=True) + EPS) * g


def _params(n_axes):
    return pltpu.CompilerParams(dimension_semantics=("arbitrary",) * n_axes,
                                vmem_limit_bytes=V7X_VMEM_LIMIT_BYTES)


def _resident(shape):
    nd = len(shape)
    return pl.BlockSpec(shape, lambda *_: (0,) * nd, pipeline_mode=pl.Buffered(1))


def _div_static(x, n):
    if n & (n - 1) == 0:
        return x >> (n.bit_length() - 1)
    return lax.div(x, n)


def _rank_rows(gate, n_valid):
    bid = lax.broadcasted_iota(jnp.int32, (gate.shape[0], 1), 0)
    rank = jnp.zeros(gate.shape, jnp.int32)
    for jp in range(n_valid):
        rowv = gate[jp:jp + 1, :]
        beats = (rowv > gate) | ((rowv == gate) & (jp < bid))
        rank = rank + jnp.where(beats, 1, 0)
    return rank, bid


def _bucket_breaks(n_buckets, max_n):
    n = np.arange(max_n + 1, dtype=np.int32)
    max_exact = n_buckets // 2
    nf = np.maximum(n, 1).astype(np.float32)
    large = max_exact + (np.log(nf / np.float32(max_exact)) / np.float32(math.log(MAX_DISTANCE / max_exact))
                         * np.float32(n_buckets - max_exact)).astype(np.int32)
    b = np.where(n < max_exact, n, np.minimum(large, n_buckets - 1))
    assert np.all(np.diff(b) >= 0)
    breaks = []
    for v in np.unique(b)[:-1]:
        breaks.append((int(v), int(np.max(n[b == v]))))
    top = int(b[-1])
    assert int(b[MOBA_BLOCK + 1]) == top
    return breaks, top


def _bias_of_dist(n, value_of_bucket, breaks, top):
    val = jnp.where(n <= breaks[-1][1], value_of_bucket(breaks[-1][0]), value_of_bucket(top))
    for bv, last in reversed(breaks[:-1]):
        val = jnp.where(n <= last, value_of_bucket(bv), val)
    return val


def _prompt_bias_kernel(rb_ref, o_ref, *, breaks, top):
    h = pl.program_id(0)
    key = lax.broadcasted_iota(jnp.int32, (MOBA_BLOCK, MOBA_BLOCK), 0)
    qry = lax.broadcasted_iota(jnp.int32, (MOBA_BLOCK, MOBA_BLOCK), 1)
    for kind in (0, 1):
        n = qry - key + kind * MOBA_BLOCK
        val = _bias_of_dist(n, lambda b: rb_ref[b, h], breaks, top) * LOG2E
        if kind == 0:
            val = jnp.where(n < 0, NEG, val)
        o_ref[0, kind] = val.astype(f32)


def _sample_bias_kernel(rbt_ref, last_ref, own_ref, far_ref, *, breaks, top, dec_t):
    ht = rbt_ref.shape[0]
    t = lax.broadcasted_iota(jnp.int32, (ht, 1), 0) & (dec_t - 1)
    col = lambda b: rbt_ref[:, b:b + 1]
    c = lax.broadcasted_iota(jnp.int32, (1, last_ref.shape[1]), 1)
    last_ref[...] = _bias_of_dist(MOBA_BLOCK + t - c, col, breaks, top)
    c = lax.broadcasted_iota(jnp.int32, (1, own_ref.shape[1]), 1)
    n = t - c
    own_ref[...] = jnp.where(n < 0, NEG, _bias_of_dist(n, col, breaks, top))
    far_ref[...] = jnp.broadcast_to(col(top), far_ref.shape)


def _layer_a_kernel(x_ref, gn_ref, win_ref, gv_ref, wmix_ref, bmix_ref, wout_ref, y_ref, *rest,
                    sub_len, emit_v):
    if emit_v:
        v_ref, vn_scr = rest
    else:
        (vn_scr,) = rest
    tm = x_ref.shape[0]
    e = gv_ref.shape[1]
    n_groups = wmix_ref.shape[0]
    gd = e // n_groups
    n_chunks = tm // CHUNK
    wide = 2 * V7X_LANES
    g_per = wide // gd

    x = x_ref[...]
    h = _rms_rows(x, gn_ref[...]).astype(bf16)

    vg = _gelu(_dot(h, win_ref[:, e:2 * e]))
    vn = _rms_rows(vg, gv_ref[...])
    if emit_v:
        v_ref[...] = vn
    vn_scr[...] = vn.astype(bf16)

    r = lax.broadcasted_iota(jnp.int32, (CHUNK, CHUNK), 0)
    c = lax.broadcasted_iota(jnp.int32, (CHUNK, CHUNK), 1)
    keep = r >= c
    if sub_len != CHUNK:
        keep = keep & ((r ^ c) < sub_len)
    bias_rows = [bmix_ref[...]] * n_chunks

    acc = jnp.zeros(y_ref.shape, f32)
    for p in range(e // wide):
        lo = p * wide
        u = _gelu(_dot(h, win_ref[:, lo:lo + wide]))
        z = _dot(h, win_ref[:, 2 * e + lo:2 * e + lo + wide])
        parts = []
        for gg in range(g_per):
            g = p * g_per + gg
            wm = jnp.where(keep, wmix_ref[g], 0.0).astype(bf16)
            rhs = jnp.concatenate(
                [vn_scr[k * CHUNK:(k + 1) * CHUNK, g * gd:(g + 1) * gd] for k in range(n_chunks)], axis=1)
            sg = _dot(wm, rhs)
            parts.append(jnp.concatenate([sg[:, k * gd:(k + 1) * gd] for k in range(n_chunks)], axis=0))
        s = jnp.concatenate(parts, axis=1) + jnp.concatenate(
            [b[:, lo:lo + wide] for b in bias_rows], axis=0)
        t = (u * s * _silu(z)).astype(bf16)
        acc = acc + _dot(t, wout_ref[lo:lo + wide, :])
    y_ref[...] = x + acc


def _layer_a(x, gn, w_in, g_v, w_mix, b_mix, w_out, *, sub_len, emit_v):
    n, d = x.shape
    e = g_v.shape[1]
    tm = TOKEN_TILE
    assert n % tm == 0 and tm % CHUNK == 0 and e % (2 * V7X_LANES) == 0
    row_spec = lambda width: pl.BlockSpec((tm, width), lambda i: (i, 0))
    out_shape = [jax.ShapeDtypeStruct((n, d), f32)]
    out_specs = [row_spec(d)]
    if emit_v:
        out_shape.append(jax.ShapeDtypeStruct((n, e), f32))
        out_specs.append(row_spec(e))
    res = pl.pallas_call(
        functools.partial(_layer_a_kernel, sub_len=sub_len, emit_v=emit_v),
        grid=(n // tm,),
        in_specs=[row_spec(d), _resident(gn.shape), _resident(w_in.shape), _resident(g_v.shape),
                  _resident(w_mix.shape), _resident(b_mix.shape), _resident(w_out.shape)],
        out_specs=out_specs,
        out_shape=out_shape,
        scratch_shapes=[pltpu.VMEM((tm, e), bf16)],
        compiler_params=_params(1),
        name="layer_a_sample" if emit_v else "layer_a_prompt",
    )(x, gn, w_in, g_v, w_mix, b_mix, w_out)
    return res if emit_v else res[0]


def _head_norm_t(a, g_ref, head_dim):
    eb, tm = a.shape
    a3 = a.reshape(eb // head_dim, head_dim, tm)
    rs = lax.rsqrt(jnp.mean(a3 * a3, axis=1, keepdims=True) + EPS)
    g = jnp.concatenate([g_ref[...]] * (tm // V7X_LANES), axis=1).reshape(a3.shape)
    return (a3 * rs * g).reshape(eb, tm)


def _head_norm_rows(a, g_ref, ind_ref, indt_ref, head_dim):
    ss = _dot((a * a).astype(bf16), ind_ref[...])
    rs = lax.rsqrt(ss * (1.0 / head_dim) + EPS)
    rs_hi = rs.astype(bf16)
    rs_lo = (rs - rs_hi.astype(f32)).astype(bf16)
    scale = _dot(rs_hi, indt_ref[...]) + _dot(rs_lo, indt_ref[...])
    return a * scale * g_ref[...]


def _proj_b_prompt_kernel(x_ref, gn_ref, wt_ref, wz_ref, gq_ref, gk_ref, *rest, head_dim, n_prev):
    prev = rest[:2 if n_prev else 0]
    kt_ref, vt_ref, qtb_ref, vtb_ref, ktok_ref, sz_ref, ksum_ref = rest[len(prev):]
    eb = wz_ref.shape[1]
    h = _rms_rows(x_ref[...], gn_ref[...]).astype(bf16)
    qt = _head_norm_t(_dot_nt(wt_ref[0:eb, :], h), gq_ref, head_dim) * (head_dim ** -0.5 * LOG2E)
    qtb_ref[...] = qt.astype(bf16)
    kt = _head_norm_t(_dot_nt(wt_ref[eb:2 * eb, :], h), gk_ref, head_dim)
    ktok = kt.T
    ktok_ref[...] = ktok.astype(bf16)
    ksum_ref[0] = jnp.sum(ktok, axis=0, keepdims=True)
    vt = _dot_nt(wt_ref[2 * eb:3 * eb, :], h)
    vtb_ref[...] = vt.astype(bf16)
    sz_ref[...] = _silu(_dot(h, wz_ref[...])).astype(bf16)
    if n_prev:
        kt_ref[0:n_prev] = prev[0][...]
        vt_ref[0:n_prev] = prev[1][...]
    kt_ref[n_prev] = kt
    vt_ref[n_prev] = vt


def _proj_b_prompt(x, gn, w_t, w_z, gq_t, gk_t, prev_kv, *, batch, seq, head_dim):
    n, d = x.shape
    eb = w_z.shape[1]
    tm = TOKEN_TILE
    assert seq % tm == 0 and tm == MOBA_BLOCK
    nt = seq // tm
    n_prev = prev_kv[0].shape[0] if prev_kv else 0
    earlier = lambda: pl.BlockSpec((n_prev, None, eb, tm), lambda i: (0, i // nt, 0, i % nt))
    final_t = lambda: pl.BlockSpec((n_prev + 1, None, eb, tm), lambda i: (0, i // nt, 0, i % nt))
    tile_t = lambda: pl.BlockSpec((None, eb, tm), lambda i: (i, 0, 0))
    rows = lambda: pl.BlockSpec((tm, eb), lambda i: (i, 0))
    stacked = jax.ShapeDtypeStruct((n_prev + 1, batch, eb, seq), f32)
    return pl.pallas_call(
        functools.partial(_proj_b_prompt_kernel, head_dim=head_dim, n_prev=n_prev),
        grid=(n // tm,),
        in_specs=[pl.BlockSpec((tm, d), lambda i: (i, 0)), _resident(gn.shape), _resident(w_t.shape),
                  _resident(w_z.shape), _resident(gq_t.shape), _resident(gk_t.shape)]
                 + [earlier() for _ in prev_kv],
        out_specs=[final_t(), final_t(), tile_t(), tile_t(), rows(), rows(),
                   pl.BlockSpec((1, 1, eb), lambda i: (i, 0, 0))],
        out_shape=[stacked, stacked,
                   jax.ShapeDtypeStruct((n // tm, eb, tm), bf16), jax.ShapeDtypeStruct((n // tm, eb, tm), bf16),
                   jax.ShapeDtypeStruct((n, eb), bf16), jax.ShapeDtypeStruct((n, eb), bf16),
                   jax.ShapeDtypeStruct((n // tm, 1, eb), f32)],
        compiler_params=_params(1),
        name="proj_b_prompt",
    )(x, gn, w_t, w_z, gq_t, gk_t, *prev_kv)


def _proj_b_sample_kernel(x_ref, gn_ref, w_ref, wt_ref, gq_ref, gk_ref, gkt_ref, ind_ref, indt_ref,
                          q_ref, kn_ref, vn_ref, sz_ref, kt_ref, vt_ref, *, head_dim):
    nseq = x_ref.shape[0]
    eb = gq_ref.shape[1]
    d = gn_ref.shape[1]
    t_per = x_ref.shape[1] // d
    x = jnp.concatenate([x_ref[:, t * d:(t + 1) * d] for t in range(t_per)], axis=0)
    h = _rms_rows(x, gn_ref[...]).astype(bf16)

    def put_rows(ref, a):
        for t in range(t_per):
            ref[:, t * eb:(t + 1) * eb] = a[t * nseq:(t + 1) * nseq].astype(ref.dtype)

    def put_t(ref, a):
        for t in range(t_per):
            ref[t] = a[:, t * nseq:(t + 1) * nseq]

    q = _head_norm_rows(_dot(h, w_ref[:, 0:eb]), gq_ref, ind_ref, indt_ref, head_dim) * (head_dim ** -0.5)
    put_rows(q_ref, q)
    put_rows(kn_ref, _head_norm_rows(_dot(h, w_ref[:, eb:2 * eb]), gk_ref, ind_ref, indt_ref, head_dim))
    put_rows(vn_ref, _dot(h, w_ref[:, 2 * eb:3 * eb]))
    put_rows(sz_ref, _silu(_dot(h, w_ref[:, 3 * eb:4 * eb])))
    put_t(kt_ref, _head_norm_t(_dot_nt(wt_ref[eb:2 * eb, :], h), gkt_ref, head_dim))
    put_t(vt_ref, _dot_nt(wt_ref[2 * eb:3 * eb, :], h))


def _proj_b_sample(x, gn, w, w_t, gq, gk, gk_t, ind, indt, *, dec_t, head_dim):
    nseq = x.shape[0]
    d = gn.shape[1]
    eb = gq.shape[1]
    assert nseq == V7X_LANES and TOKEN_TILE % nseq == 0
    t_per = TOKEN_TILE // nseq
    assert dec_t % t_per == 0
    rows = lambda: pl.BlockSpec((nseq, t_per * eb), lambda i: (0, i))
    sds = lambda dt: jax.ShapeDtypeStruct((nseq, dec_t * eb), dt)
    tsp = lambda: pl.BlockSpec((t_per, eb, nseq), lambda i: (i, 0, 0))
    tsd = jax.ShapeDtypeStruct((dec_t, eb, nseq), f32)
    return pl.pallas_call(
        functools.partial(_proj_b_sample_kernel, head_dim=head_dim),
        grid=(dec_t // t_per,),
        in_specs=[pl.BlockSpec((nseq, t_per * d), lambda i: (0, i)), _resident(gn.shape), _resident(w.shape),
                  _resident(w_t.shape), _resident(gq.shape), _resident(gk.shape), _resident(gk_t.shape),
                  _resident(ind.shape), _resident(indt.shape)],
        out_specs=[rows(), rows(), rows(), rows(), tsp(), tsp()],
        out_shape=[sds(f32), sds(f32), sds(f32), sds(f32), tsd, tsd],
        compiler_params=_params(1),
        name="proj_b_sample",
    )(x, gn, w, w_t, gq, gk, gk_t, ind, indt)


def _prompt_head_attention(c_far, hh, qt_ref, k_ref, ksum_ref, vt_ref, bias_ref, sz_ref, o_ref, s_scr, v1_scr, o_scr,
                           *, head_dim, between=None):
    blk = MOBA_BLOCK
    nb, hw, _ = qt_ref.shape
    heads_per = hw // head_dim
    row = lax.broadcasted_iota(jnp.int32, (hw, 1), 0)
    in_head = (row >= hh * head_dim) & (row < (hh + 1) * head_dim)
    lane = lax.broadcasted_iota(jnp.int32, (1, hw), 1)
    lane_in_head = (lane >= hh * head_dim) & (lane < (hh + 1) * head_dim)
    for j in range(nb):
        v1_scr[:, j * blk:(j + 1) * blk] = jnp.where(in_head, vt_ref[j], jnp.ones((hw, blk), bf16))
    nr = -(-nb // 8) * 8
    ksum = jnp.concatenate([ksum_ref[...]] + [jnp.zeros((nr - nb, hw), f32)] * (nr > nb), axis=0)
    ksum_hi = ksum.astype(bf16)
    ksum_hl = jnp.concatenate([ksum_hi, (ksum - ksum_hi.astype(f32)).astype(bf16)], axis=0)
    pair = lambda i, j: i * (i + 1) // 2 + j
    shifts = {}
    for i in range(nb):
        qm = jnp.where(in_head, qt_ref[i], jnp.zeros((hw, blk), bf16))
        maxes = []
        for j in range(i + 1):
            mx = None
            for lo in range(0, blk, KEY_ROWS):
                s = _dot(k_ref[j * blk + lo:j * blk + lo + KEY_ROWS, :], qm)
                if j == i:
                    s = s + bias_ref[0, lo:lo + KEY_ROWS, :]
                elif j == i - 1:
                    s = s + bias_ref[1, lo:lo + KEY_ROWS, :]
                s_scr[pair(i, j), lo:lo + KEY_ROWS, :] = s
                part = jnp.max(s, axis=0, keepdims=True)
                mx = part if mx is None else jnp.maximum(mx, part)
            maxes.append(mx + c_far if j < i - 1 else mx)
        chosen = [None] * i
        if i > MOBA_TOPK:
            g2 = _dot(ksum_hl, qm)
            rank, _ = _rank_rows(g2[0:nr] + g2[nr:2 * nr], i)
            chosen = [rank[j:j + 1, :] < MOBA_TOPK for j in range(i)]
        m = maxes[i]
        for j in range(i):
            m = jnp.maximum(m, maxes[j] if chosen[j] is None else jnp.where(chosen[j], maxes[j], NEG))
        for j in range(i + 1):
            shift = m - c_far if j < i - 1 else m
            if j < i and chosen[j] is not None:
                shift = jnp.where(chosen[j], shift, -NEG)
            shifts[i, j] = shift
    if between is not None:
        between()
    for i in range(nb):
        p = jnp.concatenate([jnp.exp2(s_scr[pair(i, j)] - shifts[i, j]).astype(bf16) for j in range(i + 1)], axis=0)
        acc = _dot(v1_scr[:, 0:(i + 1) * blk], p)
        den = acc[0:1, :]
        for x in range(1, heads_per):
            den = jnp.where(hh == x - 1, acc[x * head_dim:x * head_dim + 1, :], den)
        rows = slice(i * blk, (i + 1) * blk)
        t = ((acc / den).T * sz_ref[rows, :].astype(f32)).astype(bf16)
        o_scr[hh, rows, :] = jnp.where(lane_in_head, t, jnp.zeros_like(t))
        total = o_scr[0, rows, :]
        for other in range(1, heads_per):
            total = total + o_scr[other, rows, :]
        o_ref[rows, :] = total


def _sample_attention(seq_row, q_ref, kn_ref, vn_ref, sz_ref, far_ref, lastb_ref, ownb_ref, k_refs, v_refs, o_ref,
                      p_scr, *, head_dim, dec_t):
    eb, ps = k_refs[0].shape
    ht = p_scr.shape[0]
    gw = V7X_MXU_DEPTH
    g_heads = gw // head_dim
    gr = g_heads * dec_t
    n_grp = eb // gw
    ppb = MOBA_BLOCK // ps
    nblk = len(k_refs) // ppb
    pad_rows = lambda a: jnp.concatenate([a, jnp.zeros((ps - dec_t, eb), f32)], axis=0).astype(bf16)
    block = lambda refs, j: jnp.concatenate([refs[j * ppb + g][...].astype(bf16) for g in range(ppb)], axis=1)
    cols = lambda j: slice(j * MOBA_BLOCK, (j + 1) * MOBA_BLOCK)
    lane_tiles = lambda a: [a[:, q * V7X_LANES:(q + 1) * V7X_LANES] for q in range(a.shape[1] // V7X_LANES)]
    grp_rows = lambda g: slice(g * gr, (g + 1) * gr)
    grp_lanes = lambda g: slice(g * gw, (g + 1) * gw)

    def tokens(ref):
        row = ref[pl.ds(seq_row, 1), :]
        return jnp.concatenate([row[:, t * eb:(t + 1) * eb] for t in range(dec_t)], axis=0)

    lane_head = lax.broadcasted_iota(jnp.int32, (1, gw), 1) >> (head_dim.bit_length() - 1)
    row_head = lax.broadcasted_iota(jnp.int32, (gr, 1), 0) >> (dec_t.bit_length() - 1)
    q = tokens(q_ref)
    qbd = [jnp.where(row_head == lane_head, jnp.concatenate([q[:, grp_lanes(g)]] * g_heads, axis=0), 0.0).astype(bf16)
           for g in range(n_grp)]

    lane = lax.broadcasted_iota(jnp.int32, (1, V7X_LANES), 1)
    gm = jnp.zeros((ht, V7X_LANES), f32)
    for j in range(nblk):
        kb = block(k_refs, j)
        sj = jnp.concatenate([_dot(qbd[g], kb[grp_lanes(g), :]) for g in range(n_grp)], axis=0)
        p_scr[:, cols(j)] = sj
        gm = jnp.where(lane == j, jnp.sum(sj, axis=1, keepdims=True), gm)
    nrow = -(-nblk // 8) * 8
    rank, bid = _rank_rows(gm.T[0:nrow, :], nblk)
    sel = jnp.where((bid < nblk) & (rank < MOBA_TOPK), 1.0, 0.0)
    selm = jnp.concatenate([sel, jnp.zeros((V7X_LANES - nrow, ht), f32)], axis=0).T

    kn = pad_rows(tokens(kn_ref))
    s_own = jnp.concatenate([_dot_nt(qbd[g], kn[:, grp_lanes(g)]) for g in range(n_grp)], axis=0) + ownb_ref[...]
    far = jnp.concatenate([far_ref[...]] * (MOBA_BLOCK // V7X_LANES), axis=1)
    mx = s_own
    for j in range(nblk):
        sj = p_scr[:, cols(j)] + (lastb_ref[...] if j == nblk - 1 else far)
        sj = jnp.where(selm[:, j:j + 1] > 0.5, sj, NEG)
        p_scr[:, cols(j)] = sj
        for tile in lane_tiles(sj):
            mx = jnp.maximum(mx, tile)
    mx = jnp.max(mx, axis=1, keepdims=True)
    p_own = jnp.exp(s_own - mx)
    lsum = p_own
    vn = pad_rows(tokens(vn_ref))
    acc = [_dot(p_own[grp_rows(g)].astype(bf16), vn[:, grp_lanes(g)]) for g in range(n_grp)]
    for j in range(nblk):
        pj = jnp.exp(p_scr[:, cols(j)] - mx)
        for tile in lane_tiles(pj):
            lsum = lsum + tile
        pj = pj.astype(bf16)
        vb = block(v_refs, j)
        acc = [acc[g] + _dot_nt(pj[grp_rows(g)], vb[grp_lanes(g), :]) for g in range(n_grp)]
    inv = 1.0 / jnp.sum(lsum, axis=1, keepdims=True)
    parts = []
    for g in range(n_grp):
        o = acc[g] * inv[grp_rows(g)]
        part = jnp.zeros((dec_t, gw), f32)
        for hl in range(g_heads):
            part = part + jnp.where(lane_head == hl, o[hl * dec_t:(hl + 1) * dec_t, :], 0.0)
        parts.append(part)
    o_ref[0] = (jnp.concatenate(parts, axis=1) * tokens(sz_ref)).astype(bf16)


def _attn_kernel(pt_ref, rb_ref, qt_ref, k_ref, ksum_ref, vt_ref, bias_ref, szp_ref, q_ref, kn_ref, vn_ref, szs_ref,
                 far_ref, lastb_ref, ownb_ref, *rest, n_pages, head_dim, dec_t, top):
    del pt_ref
    k_refs = rest[:n_pages]
    v_refs = rest[n_pages:2 * n_pages]
    op_ref, os_ref, s_scr, v1_scr, o_scr, p_scr = rest[2 * n_pages:]
    step = pl.program_id(0)
    heads_per = qt_ref.shape[1] // head_dim
    n_heads = rb_ref.shape[1]
    head = step - _div_static(step, n_heads) * n_heads
    hh = head - _div_static(head, heads_per) * heads_per

    @pl.when(step == 0)
    def _():
        o_scr[...] = jnp.zeros(o_scr.shape, o_scr.dtype)

    seq_group = q_ref.shape[0]
    seq_row = step - _div_static(step, seq_group) * seq_group
    sample = functools.partial(_sample_attention, seq_row, q_ref, kn_ref, vn_ref, szs_ref, far_ref, lastb_ref,
                               ownb_ref, k_refs, v_refs, os_ref, p_scr, head_dim=head_dim, dec_t=dec_t)
    c_far = rb_ref[top, head] * LOG2E
    _prompt_head_attention(c_far, hh, qt_ref, k_ref, ksum_ref, vt_ref, bias_ref, szp_ref, op_ref, s_scr, v1_scr,
                           o_scr, head_dim=head_dim, between=sample)


def _attention(rel_bias, page_table, qtb, ktok, ksum, vtb, bias_tiles, szp, q, kn, vn, szs, far, lastb, ownb,
               cache_k, cache_v, *, layer, batch, seq, head_dim, top):
    n, eb = ktok.shape
    blk = MOBA_BLOCK
    nb = seq // blk
    hw = V7X_LANES
    heads_per = hw // head_dim
    n_hp = eb // hw
    assert heads_per >= 2
    n_dec = q.shape[0]
    dec_t = q.shape[1] // eb
    n_pages = page_table.shape[1]
    ps = cache_k.shape[3]
    ht = (eb // head_dim) * dec_t
    assert n_dec == batch * n_hp * heads_per
    assert MOBA_BLOCK % ps == 0 and (n_pages * ps) % MOBA_BLOCK == 0 and ps == V7X_LANES
    assert ht == V7X_LANES and dec_t & (dec_t - 1) == 0 and dec_t <= ps and n_pages * ps // MOBA_BLOCK <= ht
    assert eb % V7X_MXU_DEPTH == 0 and V7X_MXU_DEPTH % head_dim == 0 and n_dec % V7X_SUBLANES == 0
    b_of = lambda s: _div_static(s, n_hp * heads_per)
    hp_of = lambda s: _div_static(s, heads_per) - b_of(s) * n_hp
    head_of = lambda s: s - b_of(s) * (n_hp * heads_per)
    by_block = lambda a: a.reshape(batch, nb, eb, blk)
    seq_rows = lambda: pl.BlockSpec((seq, hw), lambda s, pt: (b_of(s), hp_of(s)))
    seq_t = lambda: pl.BlockSpec((None, nb, hw, blk), lambda s, pt: (b_of(s), 0, hp_of(s), 0))
    grp = V7X_SUBLANES
    per_grp = lambda: pl.BlockSpec((grp, dec_t * eb), lambda s, pt: (_div_static(s, grp), 0))
    per_s = lambda: pl.BlockSpec((1, dec_t, eb), lambda s, pt: (s, 0, 0))
    whole = lambda a: pl.BlockSpec(a.shape, lambda s, pt: (0,) * a.ndim)
    page = lambda g: pl.BlockSpec((None, None, eb, ps), lambda s, pt: (layer, pt[s * n_pages + g], 0, 0))
    grid_spec = pltpu.PrefetchScalarGridSpec(
        num_scalar_prefetch=1,
        grid=(n_dec,),
        in_specs=[pl.BlockSpec(memory_space=pltpu.SMEM), seq_t(), seq_rows(),
                  pl.BlockSpec((None, nb, hw), lambda s, pt: (b_of(s), 0, hp_of(s))), seq_t(),
                  pl.BlockSpec((None, 2, blk, blk), lambda s, pt: (head_of(s), 0, 0, 0)),
                  seq_rows(), per_grp(), per_grp(), per_grp(), per_grp(), whole(far), whole(lastb), whole(ownb)]
                 + [page(g) for g in range(n_pages)] * 2,
        out_specs=[seq_rows(), per_s()],
        scratch_shapes=[pltpu.VMEM((nb * (nb + 1) // 2, blk, blk), f32), pltpu.VMEM((hw, seq), bf16),
                        pltpu.VMEM((heads_per, seq, hw), bf16), pltpu.VMEM((ht, n_pages * ps), f32)],
    )
    return pl.pallas_call(
        functools.partial(_attn_kernel, n_pages=n_pages, head_dim=head_dim, dec_t=dec_t, top=top),
        grid_spec=grid_spec,
        out_shape=[jax.ShapeDtypeStruct((n, eb), bf16), jax.ShapeDtypeStruct((n_dec, dec_t, eb), bf16)],
        compiler_params=_params(1),
        name="attention",
    )(page_table.reshape(-1), rel_bias, by_block(qtb), ktok, ksum.reshape(batch, nb, eb), by_block(vtb), bias_tiles, szp,
      q, kn, vn, szs, far, lastb, ownb,
      *([cache_k] * n_pages), *([cache_v] * n_pages))


def _out_proj_kernel(x_ref, t_ref, w_ref, y_ref):
    y_ref[...] = x_ref[...] + _dot(t_ref[...], w_ref[...])


def _out_proj(x, t, w):
    n, d = x.shape
    tm = min(OUT_TILE, n)
    assert n % tm == 0
    return pl.pallas_call(
        _out_proj_kernel,
        grid=(n // tm,),
        in_specs=[pl.BlockSpec((tm, d), lambda i: (i, 0)), pl.BlockSpec((tm, t.shape[1]), lambda i: (i, 0)),
                  _resident(w.shape)],
        out_specs=pl.BlockSpec((tm, d), lambda i: (i, 0)),
        out_shape=jax.ShapeDtypeStruct((n, d), f32),
        compiler_params=_params(1),
        name="out_proj",
    )(x, t, w)


def kernel(x_prompt, x_sample, cache_k, cache_v, page_table, g_norm, rel_bias, w_in_a, g_v_a, w_s_a, b_s_a,
           w_out_a, w_in_b, g_q_b, g_k_b, w_out_b):
    batch, seq, d = x_prompt.shape
    n_dec, dec_t, _ = x_sample.shape
    depth = g_norm.shape[0]
    n_buckets, n_heads = rel_bias.shape
    head_dim = g_q_b.shape[1]
    eb = n_heads * head_dim
    e_a = g_v_a.shape[1]
    n_groups = w_s_a.shape[1]
    n_phys, page_size = cache_k.shape[1], cache_k.shape[2]
    past_len = page_table.shape[1] * page_size
    assert seq % MOBA_BLOCK == 0 and past_len % MOBA_BLOCK == 0 and CHUNK % dec_t == 0
    assert head_dim & (head_dim - 1) == 0 and V7X_LANES % head_dim == 0

    breaks, top = _bucket_breaks(n_buckets, 2 * MOBA_BLOCK + dec_t)
    heads_per = V7X_LANES // head_dim
    bias_tiles = pl.pallas_call(
        functools.partial(_prompt_bias_kernel, breaks=breaks, top=top),
        grid=(n_heads,),
        in_specs=[pl.BlockSpec(memory_space=pltpu.SMEM)],
        out_specs=pl.BlockSpec((1, 2, MOBA_BLOCK, MOBA_BLOCK), lambda h: (h, 0, 0, 0)),
        out_shape=jax.ShapeDtypeStruct((n_heads, 2, MOBA_BLOCK, MOBA_BLOCK), f32),
        compiler_params=_params(1),
        name="prompt_bias",
    )(rel_bias)
    ht = n_heads * dec_t
    rbt = jnp.repeat(rel_bias.T, dec_t, axis=0)
    lastb, ownb, far = pl.pallas_call(
        functools.partial(_sample_bias_kernel, breaks=breaks, top=top, dec_t=dec_t),
        out_shape=[jax.ShapeDtypeStruct((ht, MOBA_BLOCK), f32), jax.ShapeDtypeStruct((ht, page_size), f32),
                   jax.ShapeDtypeStruct((ht, V7X_LANES), f32)],
        name="sample_bias",
    )(rbt)

    head_of_lane = jnp.arange(eb, dtype=jnp.int32) // head_dim
    ind = (head_of_lane[:, None] == jnp.arange(V7X_LANES, dtype=jnp.int32)[None, :]).astype(bf16)
    indt = ind.T

    cache_kt = jnp.transpose(cache_k, (0, 1, 3, 4, 2)).reshape(cache_k.shape[0], n_phys, eb, page_size)
    cache_vt = jnp.transpose(cache_v, (0, 1, 3, 4, 2)).reshape(cache_v.shape[0], n_phys, eb, page_size)

    xp = x_prompt.reshape(batch * seq, d)
    xs = x_sample.reshape(n_dec * dec_t, d)
    prompt_kv, k_s, v_s, vch = [], [], [], []
    for i in range(depth):
        j = i // 2
        gn = g_norm[i][None, :]
        if i % 2 == 0:
            w_in = w_in_a[j].astype(bf16)
            w_out = w_out_a[j].astype(bf16)
            gv = g_v_a[j][None, :]
            gd = e_a // n_groups
            b_prompt = jnp.repeat(b_s_a[j].T, gd, axis=1)
            b_sample = jnp.repeat(jnp.tile(b_s_a[j][:, :dec_t], (1, CHUNK // dec_t)).T, gd, axis=1)
            w_sample = jnp.tile(w_s_a[j][:, :dec_t, :dec_t], (1, CHUNK // dec_t, CHUNK // dec_t))
            xp = _layer_a(xp, gn, w_in, gv, w_s_a[j], b_prompt, w_out, sub_len=CHUNK, emit_v=False)
            xs, v_rows = _layer_a(xs, gn, w_in, gv, w_sample, b_sample, w_out, sub_len=dec_t, emit_v=True)
            vch.append(v_rows.reshape(n_dec, dec_t, e_a))
        else:
            w_in = w_in_b[j].astype(bf16)
            w_t = w_in[:, :3 * eb].T
            w_z = w_in[:, 3 * eb:]
            w_out = w_out_b[j].astype(bf16)
            gq = jnp.tile(g_q_b[j], n_heads)[None, :]
            gk = jnp.tile(g_k_b[j], n_heads)[None, :]
            gq_t = jnp.broadcast_to(gq.T, (eb, V7X_LANES))
            gk_t = jnp.broadcast_to(gk.T, (eb, V7X_LANES))
            *prompt_kv, qtb, vtb, ktok, szp, ksum = _proj_b_prompt(xp, gn, w_t, w_z, gq_t, gk_t, prompt_kv,
                                                                   batch=batch, seq=seq, head_dim=head_dim)
            qs, kns, vns, szs, kts, vts = _proj_b_sample(xs.reshape(n_dec, dec_t * d), gn, w_in, w_t, gq, gk, gk_t,
                                                        ind, indt, dec_t=dec_t, head_dim=head_dim)
            tp, ts = _attention(rel_bias, page_table, qtb, ktok, ksum, vtb, bias_tiles, szp,
                                qs, kns, vns, szs, far, lastb, ownb, cache_kt, cache_vt,
                                layer=j, batch=batch, seq=seq, head_dim=head_dim, top=top)
            xp = _out_proj(xp, tp, w_out)
            xs = _out_proj(xs, ts.reshape(n_dec * dec_t, eb), w_out)
            k_s.append(kts)
            v_s.append(vts)
    prompt_out = lambda a: jnp.transpose(a.reshape(a.shape[0], batch, n_heads, head_dim, seq), (0, 1, 4, 2, 3))
    sample_out = lambda parts: jnp.transpose(
        jnp.stack(parts).reshape(len(parts), dec_t, n_heads, head_dim, n_dec), (0, 4, 1, 2, 3))
    return (xp.reshape(batch, seq, d), xs.reshape(n_dec, dec_t, d), prompt_out(prompt_kv[0]),
            prompt_out(prompt_kv[1]), sample_out(k_s), sample_out(v_s), jnp.stack(vch))
```

```python
import functools
import math

import numpy as np
import jax
import jax.numpy as jnp
from jax import lax
from jax.experimental import pallas as pl
from jax.experimental.pallas import tpu as pltpu

EPS = 1e-6
CHUNK = 128
MOBA_BLOCK = 256
MOBA_TOPK = 3
MAX_DISTANCE = 128
NEG = -1e30
LOG2E = math.log2(math.e)
V7X_LANES = 128
V7X_SUBLANES = 8
V7X_MXU_DEPTH = 256
V7X_VMEM_LIMIT_BYTES = 56 * 1024 * 1024
TOKEN_TILE = 256
OUT_TILE = 512
KEY_ROWS = 128

bf16 = jnp.bfloat16
f32 = jnp.float32


def _dot(a, b):
    return jnp.dot(a, b, preferred_element_type=f32)


def _dot_nt(a, b):
    return lax.dot_general(a, b, (((1,), (1,)), ((), ())), preferred_element_type=f32)


def _gelu(x):
    return 0.5 * x * (1.0 + lax.erf(x * (2.0 ** -0.5)))


def _silu(x):
    return x * jax.nn.sigmoid(x)


def _rms_rows(x, g):
    return x * lax.rsqrt(jnp.mean(x * x, axis=-1, keepdims=True) + EPS) * g


def _params(n_axes):
    return pltpu.CompilerParams(dimension_semantics=("arbitrary",) * n_axes,
                                vmem_limit_bytes=V7X_VMEM_LIMIT_BYTES)


def _resident(shape):
    nd = len(shape)
    return pl.BlockSpec(shape, lambda *_: (0,) * nd, pipeline_mode=pl.Buffered(1))


def _div_static(x, n):
    if n & (n - 1) == 0:
        return x >> (n.bit_length() - 1)
    return lax.div(x, n)


def _rank_rows(gate, n_valid):
    bid = lax.broadcasted_iota(jnp.int32, (gate.shape[0], 1), 0)
    rank = jnp.zeros(gate.shape, jnp.int32)
    for jp in range(n_valid):
        rowv = gate[jp:jp + 1, :]
        beats = (rowv > gate) | ((rowv == gate) & (jp < bid))
        rank = rank + jnp.where(beats, 1, 0)
    return rank, bid


def _bucket_breaks(n_buckets, max_n):
    n = np.arange(max_n + 1, dtype=np.int32)
    max_exact = n_buckets // 2
    nf = np.maximum(n, 1).astype(np.float32)
    large = max_exact + (np.log(nf / np.float32(max_exact)) / np.float32(math.log(MAX_DISTANCE / max_exact))
                         * np.float32(n_buckets - max_exact)).astype(np.int32)
    b = np.where(n < max_exact, n, np.minimum(large, n_buckets - 1))
    assert np.all(np.diff(b) >= 0)
    breaks = []
    for v in np.unique(b)[:-1]:
        breaks.append((int(v), int(np.max(n[b == v]))))
    top = int(b[-1])
    assert int(b[MOBA_BLOCK + 1]) == top
    return breaks, top


def _bias_of_dist(n, value_of_bucket, breaks, top):
    val = jnp.where(n <= breaks[-1][1], value_of_bucket(breaks[-1][0]), value_of_bucket(top))
    for bv, last in reversed(breaks[:-1]):
        val = jnp.where(n <= last, value_of_bucket(bv), val)
    return val


def _prompt_bias_kernel(rb_ref, o_ref, *, breaks, top):
    h = pl.program_id(0)
    key = lax.broadcasted_iota(jnp.int32, (MOBA_BLOCK, MOBA_BLOCK), 0)
    qry = lax.broadcasted_iota(jnp.int32, (MOBA_BLOCK, MOBA_BLOCK), 1)
    for kind in (0, 1):
        n = qry - key + kind * MOBA_BLOCK
        val = _bias_of_dist(n, lambda b: rb_ref[b, h], breaks, top) * LOG2E
        if kind == 0:
            val = jnp.where(n < 0, NEG, val)
        o_ref[0, kind] = val.astype(f32)


def _sample_bias_kernel(rbt_ref, last_ref, own_ref, far_ref, *, breaks, top, dec_t):
    ht = rbt_ref.shape[0]
    t = lax.broadcasted_iota(jnp.int32, (ht, 1), 0) & (dec_t - 1)
    col = lambda b: rbt_ref[:, b:b + 1]
    c = lax.broadcasted_iota(jnp.int32, (1, last_ref.shape[1]), 1)
    last_ref[...] = _bias_of_dist(MOBA_BLOCK + t - c, col, breaks, top)
    c = lax.broadcasted_iota(jnp.int32, (1, own_ref.shape[1]), 1)
    n = t - c
    own_ref[...] = jnp.where(n < 0, NEG, _bias_of_dist(n, col, breaks, top))
    far_ref[...] = jnp.broadcast_to(col(top), far_ref.shape)


def _layer_a_kernel(x_ref, gn_ref, win_ref, gv_ref, wmix_ref, bmix_ref, wout_ref, *rest,
                    sub_len, emit_v, pending):
    if pending:
        t_ref, wprev_ref, *rest = rest
    if emit_v:
        y_ref, v_ref, vn_scr = rest
    else:
        y_ref, vn_scr = rest
    tm = x_ref.shape[0]
    e = gv_ref.shape[1]
    n_groups = wmix_ref.shape[0]
    gd = e // n_groups
    n_chunks = tm // CHUNK
    wide = 2 * V7X_LANES
    g_per = wide // gd

    x = x_ref[...]
    if pending:
        x = x + _dot(t_ref[...], wprev_ref[...])
    h = _rms_rows(x, gn_ref[...]).astype(bf16)

    vg = _gelu(_dot(h, win_ref[:, e:2 * e]))
    vn = _rms_rows(vg, gv_ref[...])
    if emit_v:
        v_ref[...] = vn
    vn_scr[...] = vn.astype(bf16)

    r = lax.broadcasted_iota(jnp.int32, (CHUNK, CHUNK), 0)
    c = lax.broadcasted_iota(jnp.int32, (CHUNK, CHUNK), 1)
    keep = r >= c
    if sub_len != CHUNK:
        keep = keep & ((r ^ c) < sub_len)
    bias_rows = [bmix_ref[...]] * n_chunks

    acc = jnp.zeros(y_ref.shape, f32)
    for p in range(e // wide):
        lo = p * wide
        u = _gelu(_dot(h, win_ref[:, lo:lo + wide]))
        z = _dot(h, win_ref[:, 2 * e + lo:2 * e + lo + wide])
        parts = []
        for gg in range(g_per):
            g = p * g_per + gg
            wm = jnp.where(keep, wmix_ref[g], 0.0).astype(bf16)
            rhs = jnp.concatenate(
                [vn_scr[k * CHUNK:(k + 1) * CHUNK, g * gd:(g + 1) * gd] for k in range(n_chunks)], axis=1)
            sg = _dot(wm, rhs)
            parts.append(jnp.concatenate([sg[:, k * gd:(k + 1) * gd] for k in range(n_chunks)], axis=0))
        s = jnp.concatenate(parts, axis=1) + jnp.concatenate(
            [b[:, lo:lo + wide] for b in bias_rows], axis=0)
        t = (u * s * _silu(z)).astype(bf16)
        acc = acc + _dot(t, wout_ref[lo:lo + wide, :])
    y_ref[...] = x + acc


def _layer_a(x, gn, w_in, g_v, w_mix, b_mix, w_out, *, sub_len, emit_v, pending=()):
    n, d = x.shape
    e = g_v.shape[1]
    tm = TOKEN_TILE
    assert n % tm == 0 and tm % CHUNK == 0 and e % (2 * V7X_LANES) == 0
    row_spec = lambda width: pl.BlockSpec((tm, width), lambda i: (i, 0))
    out_shape = [jax.ShapeDtypeStruct((n, d), f32)]
    out_specs = [row_spec(d)]
    if emit_v:
        out_shape.append(jax.ShapeDtypeStruct((n, e), f32))
        out_specs.append(row_spec(e))
    res = pl.pallas_call(
        functools.partial(_layer_a_kernel, sub_len=sub_len, emit_v=emit_v, pending=bool(pending)),
        grid=(n // tm,),
        in_specs=[row_spec(d), _resident(gn.shape), _resident(w_in.shape), _resident(g_v.shape),
                  _resident(w_mix.shape), _resident(b_mix.shape), _resident(w_out.shape)]
                 + ([row_spec(pending[0].shape[1]), _resident(pending[1].shape)] if pending else []),
        out_specs=out_specs,
        out_shape=out_shape,
        scratch_shapes=[pltpu.VMEM((tm, e), bf16)],
        compiler_params=_params(1),
        name="layer_a_sample" if emit_v else "layer_a_prompt",
    )(x, gn, w_in, g_v, w_mix, b_mix, w_out, *pending)
    return res if emit_v else res[0]


def _head_norm_t(a, g_ref, head_dim):
    eb, tm = a.shape
    a3 = a.reshape(eb // head_dim, head_dim, tm)
    rs = lax.rsqrt(jnp.mean(a3 * a3, axis=1, keepdims=True) + EPS)
    g = jnp.concatenate([g_ref[...]] * (tm // V7X_LANES), axis=1).reshape(a3.shape)
    return (a3 * rs * g).reshape(eb, tm)


def _head_norm_rows(a, g_ref, ind_ref, indt_ref, head_dim):
    ss = _dot((a * a).astype(bf16), ind_ref[...])
    rs = lax.rsqrt(ss * (1.0 / head_dim) + EPS)
    rs_hi = rs.astype(bf16)
    rs_lo = (rs - rs_hi.astype(f32)).astype(bf16)
    scale = _dot(rs_hi, indt_ref[...]) + _dot(rs_lo, indt_ref[...])
    return a * scale * g_ref[...]


def _proj_b_prompt_kernel(x_ref, gn_ref, wt_ref, wz_ref, gq_ref, gk_ref, *rest, head_dim, n_prev):
    prev = rest[:2 if n_prev else 0]
    kt_ref, vt_ref, qtb_ref, vtb_ref, ktok_ref, sz_ref, ksum_ref = rest[len(prev):]
    eb = wz_ref.shape[1]
    h = _rms_rows(x_ref[...], gn_ref[...]).astype(bf16)
    qt = _head_norm_t(_dot_nt(wt_ref[0:eb, :], h), gq_ref, head_dim) * (head_dim ** -0.5 * LOG2E)
    qtb_ref[...] = qt.astype(bf16)
    kt = _head_norm_t(_dot_nt(wt_ref[eb:2 * eb, :], h), gk_ref, head_dim)
    ktok = kt.T
    ktok_ref[...] = ktok.astype(bf16)
    ksum_ref[0] = jnp.sum(ktok, axis=0, keepdims=True)
    vt = _dot_nt(wt_ref[2 * eb:3 * eb, :], h)
    vtb_ref[...] = vt.astype(bf16)
    sz_ref[...] = _silu(_dot(h, wz_ref[...])).astype(bf16)
    if n_prev:
        kt_ref[0:n_prev] = prev[0][...]
        vt_ref[0:n_prev] = prev[1][...]
    kt_ref[n_prev] = kt
    vt_ref[n_prev] = vt


def _proj_b_prompt(x, gn, w_t, w_z, gq_t, gk_t, prev_kv, *, batch, seq, head_dim):
    n, d = x.shape
    eb = w_z.shape[1]
    tm = TOKEN_TILE
    assert seq % tm == 0 and tm == MOBA_BLOCK
    nt = seq // tm
    n_prev = prev_kv[0].shape[0] if prev_kv else 0
    earlier = lambda: pl.BlockSpec((n_prev, None, eb, tm), lambda i: (0, i // nt, 0, i % nt))
    final_t = lambda: pl.BlockSpec((n_prev + 1, None, eb, tm), lambda i: (0, i // nt, 0, i % nt))
    tile_t = lambda: pl.BlockSpec((None, eb, tm), lambda i: (i, 0, 0))
    rows = lambda: pl.BlockSpec((tm, eb), lambda i: (i, 0))
    stacked = jax.ShapeDtypeStruct((n_prev + 1, batch, eb, seq), f32)
    return pl.pallas_call(
        functools.partial(_proj_b_prompt_kernel, head_dim=head_dim, n_prev=n_prev),
        grid=(n // tm,),
        in_specs=[pl.BlockSpec((tm, d), lambda i: (i, 0)), _resident(gn.shape), _resident(w_t.shape),
                  _resident(w_z.shape), _resident(gq_t.shape), _resident(gk_t.shape)]
                 + [earlier() for _ in prev_kv],
        out_specs=[final_t(), final_t(), tile_t(), tile_t(), rows(), rows(),
                   pl.BlockSpec((1, 1, eb), lambda i: (i, 0, 0))],
        out_shape=[stacked, stacked,
                   jax.ShapeDtypeStruct((n // tm, eb, tm), bf16), jax.ShapeDtypeStruct((n // tm, eb, tm), bf16),
                   jax.ShapeDtypeStruct((n, eb), bf16), jax.ShapeDtypeStruct((n, eb), bf16),
                   jax.ShapeDtypeStruct((n // tm, 1, eb), f32)],
        compiler_params=_params(1),
        name="proj_b_prompt",
    )(x, gn, w_t, w_z, gq_t, gk_t, *prev_kv)


def _proj_b_sample_kernel(x_ref, gn_ref, w_ref, wt_ref, gq_ref, gk_ref, gkt_ref, ind_ref, indt_ref,
                          q_ref, kn_ref, vn_ref, sz_ref, kt_ref, vt_ref, *, head_dim):
    nseq = x_ref.shape[0]
    eb = gq_ref.shape[1]
    d = gn_ref.shape[1]
    t_per = x_ref.shape[1] // d
    x = jnp.concatenate([x_ref[:, t * d:(t + 1) * d] for t in range(t_per)], axis=0)
    h = _rms_rows(x, gn_ref[...]).astype(bf16)

    def put_rows(ref, a):
        for t in range(t_per):
            ref[:, t * eb:(t + 1) * eb] = a[t * nseq:(t + 1) * nseq].astype(ref.dtype)

    def put_t(ref, a):
        for t in range(t_per):
            ref[t] = a[:, t * nseq:(t + 1) * nseq]

    q = _head_norm_rows(_dot(h, w_ref[:, 0:eb]), gq_ref, ind_ref, indt_ref, head_dim) * (head_dim ** -0.5)
    put_rows(q_ref, q)
    put_rows(kn_ref, _head_norm_rows(_dot(h, w_ref[:, eb:2 * eb]), gk_ref, ind_ref, indt_ref, head_dim))
    put_rows(vn_ref, _dot(h, w_ref[:, 2 * eb:3 * eb]))
    put_rows(sz_ref, _silu(_dot(h, w_ref[:, 3 * eb:4 * eb])))
    put_t(kt_ref, _head_norm_t(_dot_nt(wt_ref[eb:2 * eb, :], h), gkt_ref, head_dim))
    put_t(vt_ref, _dot_nt(wt_ref[2 * eb:3 * eb, :], h))


def _proj_b_sample(x, gn, w, w_t, gq, gk, gk_t, ind, indt, *, dec_t, head_dim):
    nseq = x.shape[0]
    d = gn.shape[1]
    eb = gq.shape[1]
    assert nseq == V7X_LANES and TOKEN_TILE % nseq == 0
    t_per = TOKEN_TILE // nseq
    assert dec_t % t_per == 0
    rows = lambda: pl.BlockSpec((nseq, t_per * eb), lambda i: (0, i))
    sds = lambda dt: jax.ShapeDtypeStruct((nseq, dec_t * eb), dt)
    tsp = lambda: pl.BlockSpec((t_per, eb, nseq), lambda i: (i, 0, 0))
    tsd = jax.ShapeDtypeStruct((dec_t, eb, nseq), f32)
    return pl.pallas_call(
        functools.partial(_proj_b_sample_kernel, head_dim=head_dim),
        grid=(dec_t // t_per,),
        in_specs=[pl.BlockSpec((nseq, t_per * d), lambda i: (0, i)), _resident(gn.shape), _resident(w.shape),
                  _resident(w_t.shape), _resident(gq.shape), _resident(gk.shape), _resident(gk_t.shape),
                  _resident(ind.shape), _resident(indt.shape)],
        out_specs=[rows(), rows(), rows(), rows(), tsp(), tsp()],
        out_shape=[sds(f32), sds(f32), sds(f32), sds(f32), tsd, tsd],
        compiler_params=_params(1),
        name="proj_b_sample",
    )(x, gn, w, w_t, gq, gk, gk_t, ind, indt)


def _prompt_head_attention(c_far, hh, qt_ref, k_ref, ksum_ref, vt_ref, bias_ref, sz_ref, o_ref, s_scr, v1_scr, o_scr,
                           *, head_dim, between=None):
    blk = MOBA_BLOCK
    nb, hw, _ = qt_ref.shape
    heads_per = hw // head_dim
    row = lax.broadcasted_iota(jnp.int32, (hw, 1), 0)
    in_head = (row >= hh * head_dim) & (row < (hh + 1) * head_dim)
    lane = lax.broadcasted_iota(jnp.int32, (1, hw), 1)
    lane_in_head = (lane >= hh * head_dim) & (lane < (hh + 1) * head_dim)
    for j in range(nb):
        v1_scr[:, j * blk:(j + 1) * blk] = jnp.where(in_head, vt_ref[j], jnp.ones((hw, blk), bf16))
    nr = -(-nb // 8) * 8
    ksum = jnp.concatenate([ksum_ref[...]] + [jnp.zeros((nr - nb, hw), f32)] * (nr > nb), axis=0)
    ksum_hi = ksum.astype(bf16)
    ksum_hl = jnp.concatenate([ksum_hi, (ksum - ksum_hi.astype(f32)).astype(bf16)], axis=0)
    pair = lambda i, j: i * (i + 1) // 2 + j
    shifts = {}
    for i in range(nb):
        qm = jnp.where(in_head, qt_ref[i], jnp.zeros((hw, blk), bf16))
        maxes = []
        for j in range(i + 1):
            mx = None
            for lo in range(0, blk, KEY_ROWS):
                s = _dot(k_ref[j * blk + lo:j * blk + lo + KEY_ROWS, :], qm)
                if j == i:
                    s = s + bias_ref[0, lo:lo + KEY_ROWS, :]
                elif j == i - 1:
                    s = s + bias_ref[1, lo:lo + KEY_ROWS, :]
                s_scr[pair(i, j), lo:lo + KEY_ROWS, :] = s
                part = jnp.max(s, axis=0, keepdims=True)
                mx = part if mx is None else jnp.maximum(mx, part)
            maxes.append(mx + c_far if j < i - 1 else mx)
        chosen = [None] * i
        if i > MOBA_TOPK:
            g2 = _dot(ksum_hl, qm)
            rank, _ = _rank_rows(g2[0:nr] + g2[nr:2 * nr], i)
            chosen = [rank[j:j + 1, :] < MOBA_TOPK for j in range(i)]
        m = maxes[i]
        for j in range(i):
            m = jnp.maximum(m, maxes[j] if chosen[j] is None else jnp.where(chosen[j], maxes[j], NEG))
        for j in range(i + 1):
            shift = m - c_far if j < i - 1 else m
            if j < i and chosen[j] is not None:
                shift = jnp.where(chosen[j], shift, -NEG)
            shifts[i, j] = shift
    if between is not None:
        between()
    for i in range(nb):
        p = jnp.concatenate([jnp.exp2(s_scr[pair(i, j)] - shifts[i, j]).astype(bf16) for j in range(i + 1)], axis=0)
        acc = _dot(v1_scr[:, 0:(i + 1) * blk], p)
        den = acc[0:1, :]
        for x in range(1, heads_per):
            den = jnp.where(hh == x - 1, acc[x * head_dim:x * head_dim + 1, :], den)
        rows = slice(i * blk, (i + 1) * blk)
        t = ((acc / den).T * sz_ref[rows, :].astype(f32)).astype(bf16)
        o_scr[hh, rows, :] = jnp.where(lane_in_head, t, jnp.zeros_like(t))
        total = o_scr[0, rows, :]
        for other in range(1, heads_per):
            total = total + o_scr[other, rows, :]
        o_ref[rows, :] = total


def _sample_attention(seq_row, q_ref, kn_ref, vn_ref, sz_ref, far_ref, lastb_ref, ownb_ref, k_refs, v_refs, o_ref,
                      p_scr, *, head_dim, dec_t):
    eb, ps = k_refs[0].shape
    ht = p_scr.shape[0]
    gw = V7X_MXU_DEPTH
    g_heads = gw // head_dim
    gr = g_heads * dec_t
    n_grp = eb // gw
    ppb = MOBA_BLOCK // ps
    nblk = len(k_refs) // ppb
    pad_rows = lambda a: jnp.concatenate([a, jnp.zeros((ps - dec_t, eb), f32)], axis=0).astype(bf16)
    block = lambda refs, j: jnp.concatenate([refs[j * ppb + g][...].astype(bf16) for g in range(ppb)], axis=1)
    cols = lambda j: slice(j * MOBA_BLOCK, (j + 1) * MOBA_BLOCK)
    lane_tiles = lambda a: [a[:, q * V7X_LANES:(q + 1) * V7X_LANES] for q in range(a.shape[1] // V7X_LANES)]
    grp_rows = lambda g: slice(g * gr, (g + 1) * gr)
    grp_lanes = lambda g: slice(g * gw, (g + 1) * gw)

    def tokens(ref):
        row = ref[pl.ds(seq_row, 1), :]
        return jnp.concatenate([row[:, t * eb:(t + 1) * eb] for t in range(dec_t)], axis=0)

    lane_head = lax.broadcasted_iota(jnp.int32, (1, gw), 1) >> (head_dim.bit_length() - 1)
    row_head = lax.broadcasted_iota(jnp.int32, (gr, 1), 0) >> (dec_t.bit_length() - 1)
    q = tokens(q_ref)
    qbd = [jnp.where(row_head == lane_head, jnp.concatenate([q[:, grp_lanes(g)]] * g_heads, axis=0), 0.0).astype(bf16)
           for g in range(n_grp)]

    lane = lax.broadcasted_iota(jnp.int32, (1, V7X_LANES), 1)
    gm = jnp.zeros((ht, V7X_LANES), f32)
    for j in range(nblk):
        kb = block(k_refs, j)
        sj = jnp.concatenate([_dot(qbd[g], kb[grp_lanes(g), :]) for g in range(n_grp)], axis=0)
        p_scr[:, cols(j)] = sj
        gm = jnp.where(lane == j, jnp.sum(sj, axis=1, keepdims=True), gm)
    nrow = -(-nblk // 8) * 8
    rank, bid = _rank_rows(gm.T[0:nrow, :], nblk)
    sel = jnp.where((bid < nblk) & (rank < MOBA_TOPK), 1.0, 0.0)
    selm = jnp.concatenate([sel, jnp.zeros((V7X_LANES - nrow, ht), f32)], axis=0).T

    kn = pad_rows(tokens(kn_ref))
    s_own = jnp.concatenate([_dot_nt(qbd[g], kn[:, grp_lanes(g)]) for g in range(n_grp)], axis=0) + ownb_ref[...]
    far = jnp.concatenate([far_ref[...]] * (MOBA_BLOCK // V7X_LANES), axis=1)
    mx = s_own
    for j in range(nblk):
        sj = p_scr[:, cols(j)] + (lastb_ref[...] if j == nblk - 1 else far)
        sj = jnp.where(selm[:, j:j + 1] > 0.5, sj, NEG)
        p_scr[:, cols(j)] = sj
        for tile in lane_tiles(sj):
            mx = jnp.maximum(mx, tile)
    mx = jnp.max(mx, axis=1, keepdims=True)
    p_own = jnp.exp(s_own - mx)
    lsum = p_own
    vn = pad_rows(tokens(vn_ref))
    acc = [_dot(p_own[grp_rows(g)].astype(bf16), vn[:, grp_lanes(g)]) for g in range(n_grp)]
    for j in range(nblk):
        pj = jnp.exp(p_scr[:, cols(j)] - mx)
        for tile in lane_tiles(pj):
            lsum = lsum + tile
        pj = pj.astype(bf16)
        vb = block(v_refs, j)
        acc = [acc[g] + _dot_nt(pj[grp_rows(g)], vb[grp_lanes(g), :]) for g in range(n_grp)]
    inv = 1.0 / jnp.sum(lsum, axis=1, keepdims=True)
    parts = []
    for g in range(n_grp):
        o = acc[g] * inv[grp_rows(g)]
        part = jnp.zeros((dec_t, gw), f32)
        for hl in range(g_heads):
            part = part + jnp.where(lane_head == hl, o[hl * dec_t:(hl + 1) * dec_t, :], 0.0)
        parts.append(part)
    o_ref[0] = (jnp.concatenate(parts, axis=1) * tokens(sz_ref)).astype(bf16)


def _attn_kernel(pt_ref, rb_ref, qt_ref, k_ref, ksum_ref, vt_ref, bias_ref, szp_ref, q_ref, kn_ref, vn_ref, szs_ref,
                 far_ref, lastb_ref, ownb_ref, *rest, n_pages, head_dim, dec_t, top):
    del pt_ref
    k_refs = rest[:n_pages]
    v_refs = rest[n_pages:2 * n_pages]
    op_ref, os_ref, s_scr, v1_scr, o_scr, p_scr = rest[2 * n_pages:]
    step = pl.program_id(0)
    heads_per = qt_ref.shape[1] // head_dim
    n_heads = rb_ref.shape[1]
    head = step - _div_static(step, n_heads) * n_heads
    hh = head - _div_static(head, heads_per) * heads_per

    @pl.when(step == 0)
    def _():
        o_scr[...] = jnp.zeros(o_scr.shape, o_scr.dtype)

    seq_group = q_ref.shape[0]
    seq_row = step - _div_static(step, seq_group) * seq_group
    sample = functools.partial(_sample_attention, seq_row, q_ref, kn_ref, vn_ref, szs_ref, far_ref, lastb_ref,
                               ownb_ref, k_refs, v_refs, os_ref, p_scr, head_dim=head_dim, dec_t=dec_t)
    c_far = rb_ref[top, head] * LOG2E
    _prompt_head_attention(c_far, hh, qt_ref, k_ref, ksum_ref, vt_ref, bias_ref, szp_ref, op_ref, s_scr, v1_scr,
                           o_scr, head_dim=head_dim, between=sample)


def _attention(rel_bias, page_table, qtb, ktok, ksum, vtb, bias_tiles, szp, q, kn, vn, szs, far, lastb, ownb,
               cache_k, cache_v, *, layer, batch, seq, head_dim, top):
    n, eb = ktok.shape
    blk = MOBA_BLOCK
    nb = seq // blk
    hw = V7X_LANES
    heads_per = hw // head_dim
    n_hp = eb // hw
    assert heads_per >= 2
    n_dec = q.shape[0]
    dec_t = q.shape[1] // eb
    n_pages = page_table.shape[1]
    ps = cache_k.shape[3]
    ht = (eb // head_dim) * dec_t
    assert n_dec == batch * n_hp * heads_per
    assert MOBA_BLOCK % ps == 0 and (n_pages * ps) % MOBA_BLOCK == 0 and ps == V7X_LANES
    assert ht == V7X_LANES and dec_t & (dec_t - 1) == 0 and dec_t <= ps and n_pages * ps // MOBA_BLOCK <= ht
    assert eb % V7X_MXU_DEPTH == 0 and V7X_MXU_DEPTH % head_dim == 0 and n_dec % V7X_SUBLANES == 0
    b_of = lambda s: _div_static(s, n_hp * heads_per)
    hp_of = lambda s: _div_static(s, heads_per) - b_of(s) * n_hp
    head_of = lambda s: s - b_of(s) * (n_hp * heads_per)
    by_block = lambda a: a.reshape(batch, nb, eb, blk)
    seq_rows = lambda: pl.BlockSpec((seq, hw), lambda s, pt: (b_of(s), hp_of(s)))
    seq_t = lambda: pl.BlockSpec((None, nb, hw, blk), lambda s, pt: (b_of(s), 0, hp_of(s), 0))
    grp = V7X_SUBLANES
    per_grp = lambda: pl.BlockSpec((grp, dec_t * eb), lambda s, pt: (_div_static(s, grp), 0))
    per_s = lambda: pl.BlockSpec((1, dec_t, eb), lambda s, pt: (s, 0, 0))
    whole = lambda a: pl.BlockSpec(a.shape, lambda s, pt: (0,) * a.ndim)
    page = lambda g: pl.BlockSpec((None, None, eb, ps), lambda s, pt: (layer, pt[s * n_pages + g], 0, 0))
    grid_spec = pltpu.PrefetchScalarGridSpec(
        num_scalar_prefetch=1,
        grid=(n_dec,),
        in_specs=[pl.BlockSpec(memory_space=pltpu.SMEM), seq_t(), seq_rows(),
                  pl.BlockSpec((None, nb, hw), lambda s, pt: (b_of(s), 0, hp_of(s))), seq_t(),
                  pl.BlockSpec((None, 2, blk, blk), lambda s, pt: (head_of(s), 0, 0, 0)),
                  seq_rows(), per_grp(), per_grp(), per_grp(), per_grp(), whole(far), whole(lastb), whole(ownb)]
                 + [page(g) for g in range(n_pages)] * 2,
        out_specs=[seq_rows(), per_s()],
        scratch_shapes=[pltpu.VMEM((nb * (nb + 1) // 2, blk, blk), f32), pltpu.VMEM((hw, seq), bf16),
                        pltpu.VMEM((heads_per, seq, hw), bf16), pltpu.VMEM((ht, n_pages * ps), f32)],
    )
    return pl.pallas_call(
        functools.partial(_attn_kernel, n_pages=n_pages, head_dim=head_dim, dec_t=dec_t, top=top),
        grid_spec=grid_spec,
        out_shape=[jax.ShapeDtypeStruct((n, eb), bf16), jax.ShapeDtypeStruct((n_dec, dec_t, eb), bf16)],
        compiler_params=_params(1),
        name="attention",
    )(page_table.reshape(-1), rel_bias, by_block(qtb), ktok, ksum.reshape(batch, nb, eb), by_block(vtb), bias_tiles, szp,
      q, kn, vn, szs, far, lastb, ownb,
      *([cache_k] * n_pages), *([cache_v] * n_pages))


def _out_proj_kernel(x_ref, t_ref, w_ref, y_ref):
    y_ref[...] = x_ref[...] + _dot(t_ref[...], w_ref[...])


def _out_proj(x, t, w):
    n, d = x.shape
    tm = min(OUT_TILE, n)
    assert n % tm == 0
    return pl.pallas_call(
        _out_proj_kernel,
        grid=(n // tm,),
        in_specs=[pl.BlockSpec((tm, d), lambda i: (i, 0)), pl.BlockSpec((tm, t.shape[1]), lambda i: (i, 0)),
                  _resident(w.shape)],
        out_specs=pl.BlockSpec((tm, d), lambda i: (i, 0)),
        out_shape=jax.ShapeDtypeStruct((n, d), f32),
        compiler_params=_params(1),
        name="out_proj",
    )(x, t, w)


def kernel(x_prompt, x_sample, cache_k, cache_v, page_table, g_norm, rel_bias, w_in_a, g_v_a, w_s_a, b_s_a,
           w_out_a, w_in_b, g_q_b, g_k_b, w_out_b):
    batch, seq, d = x_prompt.shape
    n_dec, dec_t, _ = x_sample.shape
    depth = g_norm.shape[0]
    n_buckets, n_heads = rel_bias.shape
    head_dim = g_q_b.shape[1]
    eb = n_heads * head_dim
    e_a = g_v_a.shape[1]
    n_groups = w_s_a.shape[1]
    n_phys, page_size = cache_k.shape[1], cache_k.shape[2]
    past_len = page_table.shape[1] * page_size
    assert seq % MOBA_BLOCK == 0 and past_len % MOBA_BLOCK == 0 and CHUNK % dec_t == 0
    assert head_dim & (head_dim - 1) == 0 and V7X_LANES % head_dim == 0

    breaks, top = _bucket_breaks(n_buckets, 2 * MOBA_BLOCK + dec_t)
    heads_per = V7X_LANES // head_dim
    bias_tiles = pl.pallas_call(
        functools.partial(_prompt_bias_kernel, breaks=breaks, top=top),
        grid=(n_heads,),
        in_specs=[pl.BlockSpec(memory_space=pltpu.SMEM)],
        out_specs=pl.BlockSpec((1, 2, MOBA_BLOCK, MOBA_BLOCK), lambda h: (h, 0, 0, 0)),
        out_shape=jax.ShapeDtypeStruct((n_heads, 2, MOBA_BLOCK, MOBA_BLOCK), f32),
        compiler_params=_params(1),
        name="prompt_bias",
    )(rel_bias)
    ht = n_heads * dec_t
    rbt = jnp.repeat(rel_bias.T, dec_t, axis=0)
    lastb, ownb, far = pl.pallas_call(
        functools.partial(_sample_bias_kernel, breaks=breaks, top=top, dec_t=dec_t),
        out_shape=[jax.ShapeDtypeStruct((ht, MOBA_BLOCK), f32), jax.ShapeDtypeStruct((ht, page_size), f32),
                   jax.ShapeDtypeStruct((ht, V7X_LANES), f32)],
        name="sample_bias",
    )(rbt)

    head_of_lane = jnp.arange(eb, dtype=jnp.int32) // head_dim
    ind = (head_of_lane[:, None] == jnp.arange(V7X_LANES, dtype=jnp.int32)[None, :]).astype(bf16)
    indt = ind.T

    cache_kt = jnp.transpose(cache_k, (0, 1, 3, 4, 2)).reshape(cache_k.shape[0], n_phys, eb, page_size)
    cache_vt = jnp.transpose(cache_v, (0, 1, 3, 4, 2)).reshape(cache_v.shape[0], n_phys, eb, page_size)

    xp = x_prompt.reshape(batch * seq, d)
    xs = x_sample.reshape(n_dec * dec_t, d)
    prompt_kv, k_s, v_s, vch = [], [], [], []
    pend_p, pend_s = (), ()
    for i in range(depth):
        j = i // 2
        gn = g_norm[i][None, :]
        if i % 2 == 0:
            w_in = w_in_a[j].astype(bf16)
            w_out = w_out_a[j].astype(bf16)
            gv = g_v_a[j][None, :]
            gd = e_a // n_groups
            b_prompt = jnp.repeat(b_s_a[j].T, gd, axis=1)
            b_sample = jnp.repeat(jnp.tile(b_s_a[j][:, :dec_t], (1, CHUNK // dec_t)).T, gd, axis=1)
            w_sample = jnp.tile(w_s_a[j][:, :dec_t, :dec_t], (1, CHUNK // dec_t, CHUNK // dec_t))
            xp = _layer_a(xp, gn, w_in, gv, w_s_a[j], b_prompt, w_out, sub_len=CHUNK, emit_v=False, pending=pend_p)
            xs, v_rows = _layer_a(xs, gn, w_in, gv, w_sample, b_sample, w_out, sub_len=dec_t, emit_v=True,
                                  pending=pend_s)
            pend_p, pend_s = (), ()
            vch.append(v_rows.reshape(n_dec, dec_t, e_a))
        else:
            w_in = w_in_b[j].astype(bf16)
            w_t = w_in[:, :3 * eb].T
            w_z = w_in[:, 3 * eb:]
            w_out = w_out_b[j].astype(bf16)
            gq = jnp.tile(g_q_b[j], n_heads)[None, :]
            gk = jnp.tile(g_k_b[j], n_heads)[None, :]
            gq_t = jnp.broadcast_to(gq.T, (eb, V7X_LANES))
            gk_t = jnp.broadcast_to(gk.T, (eb, V7X_LANES))
            *prompt_kv, qtb, vtb, ktok, szp, ksum = _proj_b_prompt(xp, gn, w_t, w_z, gq_t, gk_t, prompt_kv,
                                                                   batch=batch, seq=seq, head_dim=head_dim)
            qs, kns, vns, szs, kts, vts = _proj_b_sample(xs.reshape(n_dec, dec_t * d), gn, w_in, w_t, gq, gk, gk_t,
                                                        ind, indt, dec_t=dec_t, head_dim=head_dim)
            tp, ts = _attention(rel_bias, page_table, qtb, ktok, ksum, vtb, bias_tiles, szp,
                                qs, kns, vns, szs, far, lastb, ownb, cache_kt, cache_vt,
                                layer=j, batch=batch, seq=seq, head_dim=head_dim, top=top)
            if i + 1 < depth and (i + 1) % 2 == 0:
                pend_p, pend_s = (tp, w_out), (ts.reshape(n_dec * dec_t, eb), w_out)
            else:
                xp = _out_proj(xp, tp, w_out)
                xs = _out_proj(xs, ts.reshape(n_dec * dec_t, eb), w_out)
            k_s.append(kts)
            v_s.append(vts)
    prompt_out = lambda a: jnp.transpose(a.reshape(a.shape[0], batch, n_heads, head_dim, seq), (0, 1, 4, 2, 3))
    sample_out = lambda parts: jnp.transpose(
        jnp.stack(parts).reshape(len(parts), dec_t, n_heads, head_dim, n_dec), (0, 4, 1, 2, 3))
    return (xp.reshape(batch, seq, d), xs.reshape(n_dec, dec_t, d), prompt_out(prompt_kv[0]),
            prompt_out(prompt_kv[1]), sample_out(k_s), sample_out(v_s), jnp.stack(vch))
```
